```python
import math
import jax
import jax.numpy as jnp
from jax import lax
import numpy as np

D_MODEL = 1024
BATCH = 2
SEQ = 16384
DEPTH = 2

HEAD_DIM = 64
EPS = 1e-6
A_HEADS = 4
A_PATTERNS = ((128, 1), (512, 4), (2048, 16))
A_BLOCK = 128
ROPE_THETA = 500000.0
ROPE_DIM = HEAD_DIM // 4
R_HEADS = 4
R_QK_DIM = 64
R_V_DIM = 128
R_CHUNK = 128
R_ROPE_THETA = 10000.0
S_CHANNELS = 256
S_GROUP = 16
S_GROUPS = S_CHANNELS // S_GROUP
S_STATE = 64
A_WIDTH = A_HEADS * HEAD_DIM
R_WIDTH = R_HEADS * R_V_DIM
MIX_WIDTH = A_WIDTH + R_WIDTH + S_CHANNELS
IN_SPLITS = (A_WIDTH, A_WIDTH, A_WIDTH, R_HEADS * R_QK_DIM, R_HEADS * R_QK_DIM, R_WIDTH, R_WIDTH, S_CHANNELS)
IN_COLS = sum(IN_SPLITS)
D_FF = 2816
N_EXPERTS = 8
TOP_K = 2
D_FF_EXPERT = 3584

kernel_name = "hybrid_dilated_retention_s5_moe_block"


def rms_norm(x, g):
    xf = x.astype(jnp.float32)
    y = xf * lax.rsqrt(jnp.mean(xf * xf, axis=-1, keepdims=True) + EPS)
    return (y * g.astype(jnp.float32)).astype(x.dtype)


def rotary(x, pos, rot_dim, theta):
    half = rot_dim // 2
    inv = jnp.power(theta, -jnp.arange(half, dtype=jnp.float32) * 2.0 / rot_dim)
    ang = pos.astype(jnp.float32)[:, None] * inv[None, :]
    cos = jnp.cos(ang)[:, None, :]
    sin = jnp.sin(ang)[:, None, :]
    xr = x[..., :rot_dim].astype(jnp.float32)
    x1, x2 = xr[..., :half], xr[..., half:]
    rot = jnp.concatenate([x1 * cos - x2 * sin, x2 * cos + x1 * sin], axis=-1).astype(x.dtype)
    return jnp.concatenate([rot, x[..., rot_dim:]], axis=-1)


def dilated_window_attention(q, k, v, dilation):
    b, h, s, hd = q.shape
    span = A_BLOCK * dilation
    s_pad = -(-s // span) * span
    lr = s_pad // dilation
    nb = lr // A_BLOCK

    def to_blocks(t):
        t = jnp.pad(t, ((0, 0), (0, 0), (0, s_pad - s), (0, 0)))
        t = t.reshape(b, h, lr, dilation, hd).transpose(0, 1, 3, 2, 4)
        return t.reshape(b, h, dilation, nb, A_BLOCK, hd)

    def with_prev(t):
        prev = jnp.pad(t, ((0, 0), (0, 0), (0, 0), (1, 0), (0, 0), (0, 0)))[:, :, :, :-1]
        return jnp.concatenate([prev, t], axis=4)

    qb = to_blocks(q)
    kk = with_prev(to_blocks(k))
    vv = with_prev(to_blocks(v))
    scores = jnp.einsum('bhrnid,bhrnjd->bhrnij', qb, kk) * (hd ** -0.5)
    i = jnp.arange(A_BLOCK)[:, None]
    j = jnp.arange(2 * A_BLOCK)[None, :]
    steps = A_BLOCK + i - j
    band = (steps >= 0) & (steps <= A_BLOCK)
    blk = jnp.arange(nb)[:, None, None]
    mask = band[None] & ((blk > 0) | (j >= A_BLOCK)[None])
    scores = jnp.where(mask, scores, -jnp.inf)
    m = jnp.max(scores, axis=-1, keepdims=True)
    p = jnp.exp(scores - m)
    l = jnp.sum(p, axis=-1, keepdims=True)
    o = jnp.einsum('bhrnij,bhrnjd->bhrnid', p / l, vv)
    lse = (m + jnp.log(l))[..., 0]

    def from_blocks(t):
        t = t.reshape(b, h, dilation, lr, *t.shape[5:])
        t = jnp.moveaxis(t, 2, 3)
        return t.reshape(b, h, s_pad, *t.shape[4:])[:, :, :s]

    return from_blocks(o), from_blocks(lse)


def mixer_dilated(q, k, v, q_g, k_g, pos):
    b, s, _ = q.shape
    shp = (b, s, A_HEADS, HEAD_DIM)
    q = rotary(rms_norm(q.reshape(shp), q_g), pos, ROPE_DIM, ROPE_THETA)
    k = rotary(rms_norm(k.reshape(shp), k_g), pos, ROPE_DIM, ROPE_THETA)
    v = v.reshape(shp)
    qh, kh, vh = [t.astype(jnp.float32).transpose(0, 2, 1, 3) for t in (q, k, v)]
    outs, lses = [], []
    for _, dil in A_PATTERNS:
        o_p, lse_p = dilated_window_attention(qh, kh, vh, dil)
        outs.append(o_p)
        lses.append(lse_p)
    w = jax.nn.softmax(jnp.stack(lses, axis=0), axis=0)
    o = jnp.einsum('pbhs,pbhsd->bhsd', w, jnp.stack(outs, axis=0))
    return o.transpose(0, 2, 1, 3).reshape(b, s, A_WIDTH).astype(q.dtype)


def mixer_retention(q, k, v, g, gn_g, pos):
    f32 = jnp.float32
    b, s, _ = q.shape
    dtype = q.dtype
    q = rotary(q.reshape(b, s, R_HEADS, R_QK_DIM), pos, R_QK_DIM, R_ROPE_THETA).astype(f32)
    k = (rotary(k.reshape(b, s, R_HEADS, R_QK_DIM), pos, R_QK_DIM, R_ROPE_THETA).astype(f32) * (R_QK_DIM ** -0.5))
    v = v.astype(f32).reshape(b, s, R_HEADS, R_V_DIM)
    log_gamma = jnp.log1p(-jnp.exp2(-5.0 - jnp.arange(R_HEADS, dtype=f32)))
    n = s // R_CHUNK
    qc = q.reshape(b, n, R_CHUNK, R_HEADS, R_QK_DIM)
    kc = k.reshape(b, n, R_CHUNK, R_HEADS, R_QK_DIM)
    vc = v.reshape(b, n, R_CHUNK, R_HEADS, R_V_DIM)
    idx = jnp.arange(R_CHUNK, dtype=f32)
    diff = idx[:, None] - idx[None, :]
    decay = jnp.where(diff >= 0, jnp.exp(log_gamma[:, None, None] * jnp.maximum(diff, 0.0)), 0.0)
    inner = jnp.einsum('bnihd,bnjhd->bnhij', qc, kc) * decay
    inner = jnp.einsum('bnhij,bnjhe->bnihe', inner, vc)
    zeta = jnp.exp(log_gamma[:, None] * (R_CHUNK - 1.0 - idx))
    xi = jnp.exp(log_gamma[:, None] * (idx + 1.0))
    kv = jnp.einsum('bnjhd,bnjhe,hj->nbhde', kc, vc, zeta)
    chunk_decay = jnp.exp(log_gamma * R_CHUNK)[None, :, None, None]

    def step(state, kv_n):
        return chunk_decay * state + kv_n, state

    _, prev = lax.scan(step, jnp.zeros((b, R_HEADS, R_QK_DIM, R_V_DIM), f32), kv)
    cross = jnp.einsum('bnihd,nbhde,hi->bnihe', qc, prev, xi)
    y = (inner + cross).reshape(b, s, R_HEADS, R_V_DIM)
    mu = jnp.mean(y, axis=-1, keepdims=True)
    var = jnp.mean(jnp.square(y - mu), axis=-1, keepdims=True)
    y = ((y - mu) * lax.rsqrt(var + 1e-5)).reshape(b, s, R_WIDTH) * gn_g.astype(f32)
    return (jax.nn.silu(g.astype(f32)) * y).astype(dtype)


def mixer_ssm(u, a_re, a_im, b_re, b_im, c_re, c_im, d_skip, log_dt, glu_w, glu_b):
    f32 = jnp.float32
    bsz, s, _ = u.shape
    uf = u.astype(f32).reshape(bsz, s, S_GROUPS, S_GROUP)
    dt = jnp.exp(log_dt.astype(f32))[:, None]
    lam_re = a_re.astype(f32)
    lam_im = a_im.astype(f32)
    mag = jnp.exp(lam_re * dt)
    abar_re = mag * jnp.cos(lam_im * dt)
    abar_im = mag * jnp.sin(lam_im * dt)
    den = lam_re * lam_re + lam_im * lam_im
    nr = abar_re - 1.0
    ni = abar_im
    f_re = ((nr * lam_re + ni * lam_im) / den)[..., None]
    f_im = ((ni * lam_re - nr * lam_im) / den)[..., None]
    br = b_re.astype(f32)
    bi = b_im.astype(f32)
    bb_re = f_re * br - f_im * bi
    bb_im = f_re * bi + f_im * br
    x_re = jnp.einsum('bsgc,gpc->bsgp', uf, bb_re)
    x_im = jnp.einsum('bsgc,gpc->bsgp', uf, bb_im)
    shp = (bsz, s, S_GROUPS, S_STATE)
    at_re = jnp.broadcast_to(abar_re, shp)
    at_im = jnp.broadcast_to(abar_im, shp)

    def combine(e1, e2):
        a1r, a1i, b1r, b1i = e1
        a2r, a2i, b2r, b2i = e2
        return (a2r * a1r - a2i * a1i,
                a2r * a1i + a2i * a1r,
                a2r * b1r - a2i * b1i + b2r,
                a2r * b1i + a2i * b1r + b2i)

    _, _, h_re, h_im = lax.associative_scan(combine, (at_re, at_im, x_re, x_im), axis=1)
    y = (jnp.einsum('gcp,bsgp->bsgc', c_re.astype(f32), h_re)
         - jnp.einsum('gcp,bsgp->bsgc', c_im.astype(f32), h_im)
         + d_skip.astype(f32).reshape(S_GROUPS, S_GROUP) * uf)
    z = jax.nn.gelu(y.reshape(bsz, s, S_CHANNELS))
    out = z * jax.nn.sigmoid(z @ glu_w.astype(f32) + glu_b.astype(f32))
    return out.astype(u.dtype)


def swiglu(x, w_gate, w_up, w_down):
    return (jax.nn.silu(x @ w_gate) * (x @ w_up)) @ w_down


def moe_swiglu(x, router, w_gate, w_up, w_down):
    b, s, d = x.shape
    t = x.reshape(b * s, d)
    probs = jax.nn.softmax((t @ router).astype(jnp.float32), axis=-1)
    top_p, top_i = lax.top_k(probs, TOP_K)
    top_p = top_p / jnp.sum(top_p, axis=-1, keepdims=True)
    gates = jnp.sum(jax.nn.one_hot(top_i, N_EXPERTS, dtype=jnp.float32) * top_p[..., None], axis=1)
    out = jnp.zeros((b * s, d), jnp.float32)
    for e in range(N_EXPERTS):
        out = out + gates[:, e:e + 1] * swiglu(t, w_gate[e], w_up[e], w_down[e]).astype(jnp.float32)
    return out.astype(x.dtype).reshape(b, s, d)


def setup_inputs(seed: int = 0) -> dict:
    key = jax.random.key(seed)
    ks = iter(jax.random.split(key, 32))
    f32 = jnp.float32
    L = DEPTH
    NE = (DEPTH + 1) // 2
    NO = DEPTH // 2
    G, P = S_GROUPS, S_STATE

    def nrm(shape, scale):
        return jax.random.normal(next(ks), shape, f32) * scale

    return {
        "x": nrm((BATCH, SEQ, D_MODEL), 1.0),
        "norm1_g": 1.0 + nrm((L, D_MODEL), 0.02),
        "w_in": nrm((L, D_MODEL, IN_COLS), D_MODEL ** -0.5),
        "q_norm_g": 1.0 + nrm((L, HEAD_DIM), 0.02),
        "k_norm_g": 1.0 + nrm((L, HEAD_DIM), 0.02),
        "ret_gn_g": 1.0 + nrm((L, R_WIDTH), 0.02),
        "ssm_a_re": -0.5 + nrm((L, G, P), 0.01),
        "ssm_a_im": jnp.pi * jnp.arange(P, dtype=f32)[None, None, :] + nrm((L, G, P), 0.01),
        "ssm_b_re": nrm((L, G, P, S_GROUP), (2 * S_GROUP) ** -0.5),
        "ssm_b_im": nrm((L, G, P, S_GROUP), (2 * S_GROUP) ** -0.5),
        "ssm_c_re": nrm((L, G, S_GROUP, P), 0.5),
        "ssm_c_im": nrm((L, G, S_GROUP, P), 0.5),
        "ssm_d": nrm((L, S_CHANNELS), 0.5),
        "ssm_log_dt": jax.random.uniform(next(ks), (L, G), f32, math.log(1e-3), math.log(1e-1)),
        "ssm_glu_w": nrm((L, S_CHANNELS, S_CHANNELS), S_CHANNELS ** -0.5),
        "ssm_glu_b": nrm((L, S_CHANNELS), 0.01),
        "w_out": nrm((L, MIX_WIDTH, D_MODEL), MIX_WIDTH ** -0.5),
        "norm2_g": 1.0 + nrm((L, D_MODEL), 0.02),
        "ffn_w_gate": nrm((NE, D_MODEL, D_FF), D_MODEL ** -0.5),
        "ffn_w_up": nrm((NE, D_MODEL, D_FF), D_MODEL ** -0.5),
        "ffn_w_down": nrm((NE, D_FF, D_MODEL), D_FF ** -0.5),
        "moe_router": nrm((NO, D_MODEL, N_EXPERTS), D_MODEL ** -0.5),
        "moe_w_gate": nrm((NO, N_EXPERTS, D_MODEL, D_FF_EXPERT), D_MODEL ** -0.5),
        "moe_w_up": nrm((NO, N_EXPERTS, D_MODEL, D_FF_EXPERT), D_MODEL ** -0.5),
        "moe_w_down": nrm((NO, N_EXPERTS, D_FF_EXPERT, D_MODEL), D_FF_EXPERT ** -0.5),
    }


def reference(x, norm1_g, w_in, q_norm_g, k_norm_g, ret_gn_g, ssm_a_re, ssm_a_im, ssm_b_re, ssm_b_im,
              ssm_c_re, ssm_c_im, ssm_d, ssm_log_dt, ssm_glu_w, ssm_glu_b, w_out, norm2_g,
              ffn_w_gate, ffn_w_up, ffn_w_down, moe_router, moe_w_gate, moe_w_up, moe_w_down):
    s = x.shape[1]
    pos = jnp.arange(s, dtype=jnp.int32)
    cuts = list(np.cumsum(IN_SPLITS)[:-1])
    h = x
    for layer in range(DEPTH):
        xn = rms_norm(h, norm1_g[layer])
        proj = xn @ w_in[layer]
        qa, ka, va, qr, kr, vr, gr, us = jnp.split(proj, cuts, axis=-1)
        ya = mixer_dilated(qa, ka, va, q_norm_g[layer], k_norm_g[layer], pos)
        yr = mixer_retention(qr, kr, vr, gr, ret_gn_g[layer], pos)
        yc = mixer_ssm(us, ssm_a_re[layer], ssm_a_im[layer], ssm_b_re[layer], ssm_b_im[layer],
                       ssm_c_re[layer], ssm_c_im[layer], ssm_d[layer], ssm_log_dt[layer],
                       ssm_glu_w[layer], ssm_glu_b[layer])
        mixed = jnp.concatenate([ya, yr, yc], axis=-1)
        h = h + mixed @ w_out[layer]
        xn2 = rms_norm(h, norm2_g[layer])
        if layer % 2 == 0:
            i = layer // 2
            h = h + swiglu(xn2, ffn_w_gate[i], ffn_w_up[i], ffn_w_down[i])
        else:
            i = layer // 2
            h = h + moe_swiglu(xn2, moe_router[i], moe_w_gate[i], moe_w_up[i], moe_w_down[i])
    return h
```

```python
import functools
import math

import jax
import jax.numpy as jnp
import numpy as np
from jax import lax
from jax.experimental import pallas as pl
from jax.experimental.pallas import tpu as pltpu

F32 = jnp.float32
BF16 = jnp.bfloat16

LANES = 128
EPS = 1e-6
HEAD_DIM = 64
A_HEADS = 4
A_BLOCK = 128
A_DILATIONS = (1, 4, 16)
A_SPAN = A_BLOCK * max(A_DILATIONS)
ROPE_THETA = 500000.0
ROPE_DIM = HEAD_DIM // 4
R_HEADS = 4
R_QK_DIM = 64
R_V_DIM = 128
R_ROPE_THETA = 10000.0
S_CHANNELS = 256
S_GROUP = 16
S_GROUPS = S_CHANNELS // S_GROUP
S_STATE = 64
S_CHUNK = 32
S_PAIRS = S_GROUPS // 2
A_WIDTH = A_HEADS * HEAD_DIM
R_WIDTH = R_HEADS * R_V_DIM
N_EXPERTS = 8
NEG_BIG = -1e30

VMEM_LIMIT = 56 * 1024 * 1024


def _cparams(*sem):
    return pltpu.CompilerParams(dimension_semantics=sem, vmem_limit_bytes=VMEM_LIMIT)


def _sigmoid(x):
    return 1.0 / (1.0 + jnp.exp(-x))


def _rope_tables(seq, rot_dim, theta):
    half = rot_dim // 2
    inv = jnp.power(theta, -jnp.arange(half, dtype=F32) * 2.0 / rot_dim)
    ang = jnp.arange(seq, dtype=jnp.int32).astype(F32)[:, None] * inv[None, :]
    d = np.arange(LANES) % HEAD_DIM
    idx = jnp.asarray(d % half)
    cos = jnp.cos(ang)[:, idx]
    sin = jnp.sin(ang)[:, idx]
    in_rot = jnp.asarray(d < rot_dim)[None, :]
    first = jnp.asarray(d < half)[None, :]
    cos_t = jnp.where(in_rot, cos, 1.0)
    sin_t = jnp.where(in_rot, jnp.where(first, -sin, sin), 0.0)
    return cos_t.astype(F32), sin_t.astype(F32)


def _rope_slab(x, cos_t, sin_t, half):
    lane = lax.broadcasted_iota(jnp.int32, x.shape, 1)
    fwd = pltpu.roll(x, LANES - half, 1)
    bwd = pltpu.roll(x, half, 1)
    partner = jnp.where((lane % (2 * half)) < half, fwd, bwd)
    return x * cos_t + partner * sin_t


def _head_rms_slab(x, g):
    lane = lax.broadcasted_iota(jnp.int32, x.shape, 1)
    lo = lane < HEAD_DIM
    x2 = x * x
    s0 = jnp.sum(jnp.where(lo, x2, 0.0), axis=-1, keepdims=True)
    s1 = jnp.sum(jnp.where(lo, 0.0, x2), axis=-1, keepdims=True)
    ms = jnp.where(lo, s0, s1) * (1.0 / HEAD_DIM)
    return x * lax.rsqrt(ms + EPS) * g


def _in_proj_kernel(x_ref, g_ref, w_ref, qg_ref, kg_ref, ca_ref, sa_ref, cr_ref, sr_ref,
                    qa_ref, ka_ref, va_ref, qr_ref, kr_ref, vr_ref, gr_ref, us_ref):
    x = x_ref[...]
    ms = jnp.mean(x * x, axis=-1, keepdims=True)
    xn = (x * lax.rsqrt(ms + EPS) * g_ref[...]).astype(BF16)

    def proj(c0, n):
        return jnp.dot(xn, w_ref[:, c0:c0 + n], preferred_element_type=F32)

    ca, sa, cr, sr = ca_ref[...], sa_ref[...], cr_ref[...], sr_ref[...]
    qa = proj(0, A_WIDTH)
    ka = proj(A_WIDTH, A_WIDTH)
    for s in range(A_WIDTH // LANES):
        sl = slice(s * LANES, (s + 1) * LANES)
        qn = _rope_slab(_head_rms_slab(qa[:, sl], qg_ref[...]), ca, sa, ROPE_DIM // 2)
        qa_ref[:, sl] = qn * (HEAD_DIM ** -0.5)
        ka_ref[:, sl] = _rope_slab(_head_rms_slab(ka[:, sl], kg_ref[...]), ca, sa, ROPE_DIM // 2)
    va_ref[...] = proj(2 * A_WIDTH, A_WIDTH)
    c0 = 3 * A_WIDTH
    rqk = R_HEADS * R_QK_DIM
    qr = proj(c0, rqk)
    kr = proj(c0 + rqk, rqk)
    for s in range(rqk // LANES):
        sl = slice(s * LANES, (s + 1) * LANES)
        qr_ref[:, sl] = _rope_slab(qr[:, sl], cr, sr, R_QK_DIM // 2).astype(BF16)
        kr_ref[:, sl] = (_rope_slab(kr[:, sl], cr, sr, R_QK_DIM // 2) * (R_QK_DIM ** -0.5)).astype(BF16)
    c0 += 2 * rqk
    vr_ref[...] = proj(c0, R_WIDTH).astype(BF16)
    gr_ref[...] = proj(c0 + R_WIDTH, R_WIDTH).astype(BF16)
    us_ref[...] = proj(c0 + 2 * R_WIDTH, S_CHANNELS)


def _in_proj(h2d, seq, norm_g, w_in, q_g, k_g, tabs, tm):
    t, d = h2d.shape
    n_cols = w_in.shape[1]
    nt_seq = seq // tm
    row = lambda i: (i, 0)
    fixed = lambda i: (0, 0)
    tab = lambda i: (i % nt_seq, 0)
    out_shapes = (
        jax.ShapeDtypeStruct((t, A_WIDTH), F32), jax.ShapeDtypeStruct((t, A_WIDTH), F32),
        jax.ShapeDtypeStruct((t, A_WIDTH), F32),
        jax.ShapeDtypeStruct((t, R_HEADS * R_QK_DIM), BF16), jax.ShapeDtypeStruct((t, R_HEADS * R_QK_DIM), BF16),
        jax.ShapeDtypeStruct((t, R_WIDTH), BF16), jax.ShapeDtypeStruct((t, R_WIDTH), BF16),
        jax.ShapeDtypeStruct((t, S_CHANNELS), F32))
    return pl.pallas_call(
        _in_proj_kernel,
        grid=(t // tm,),
        in_specs=[pl.BlockSpec((tm, d), row), pl.BlockSpec((1, d), fixed), pl.BlockSpec((d, n_cols), fixed),
                  pl.BlockSpec((1, LANES), fixed), pl.BlockSpec((1, LANES), fixed)]
                 + [pl.BlockSpec((tm, LANES), tab)] * 4,
        out_specs=[pl.BlockSpec((tm, s.shape[1]), row) for s in out_shapes],
        out_shape=out_shapes,
        compiler_params=_cparams("arbitrary"),
        name="in_proj",
    )(h2d, norm_g.reshape(1, d), w_in, jnp.tile(q_g, 2).reshape(1, LANES), jnp.tile(k_g, 2).reshape(1, LANES), *tabs)


def _attn_kernel(q_ref, k_ref, v_ref, o_ref, kbuf, vbuf, acc, mrun, lrun):
    i = pl.program_id(2)
    span = A_SPAN

    @pl.when(i == 0)
    def _():
        kbuf[0:span, :] = jnp.zeros((span, LANES), F32)
        vbuf[0:span, :] = jnp.zeros((span, LANES), F32)

    @pl.when(i > 0)
    def _():
        kbuf[0:span, :] = kbuf[span:2 * span, :]
        vbuf[0:span, :] = vbuf[span:2 * span, :]

    kbuf[span:2 * span, :] = k_ref[...]
    vbuf[span:2 * span, :] = v_ref[...]

    qi = lax.broadcasted_iota(jnp.int32, (A_BLOCK, 2 * A_BLOCK), 0)
    kj = lax.broadcasted_iota(jnp.int32, (A_BLOCK, 2 * A_BLOCK), 1)
    lane = lax.broadcasted_iota(jnp.int32, (A_BLOCK, LANES), 1)
    lo = lane < HEAD_DIM

    def rows(start, n, d):
        if d == 1:
            return pl.ds(pl.multiple_of(start, A_BLOCK), n)
        return pl.ds(start, n, stride=d)

    for d in A_DILATIONS:
        n_blk = span // A_BLOCK

        def body(blk, carry, d=d):
            if d == 1:
                sp, r = blk, 0
            elif d * A_BLOCK == span:
                sp, r = 0, blk
            else:
                sp, r = blk // d, blk % d
            q0 = sp * (A_BLOCK * d) + r
            qb = q_ref[rows(q0, A_BLOCK, d), :]
            k0 = span + q0 - A_BLOCK * d
            kb = kbuf[rows(k0, 2 * A_BLOCK, d), :].astype(BF16)
            vb = vbuf[rows(k0, 2 * A_BLOCK, d), :].astype(BF16)
            first_key = jnp.where(jnp.logical_or(i > 0, sp > 0), 0, A_BLOCK)
            valid = (kj >= jnp.maximum(qi, first_key)) & (kj <= qi + A_BLOCK)
            ms, ls, os_ = [], [], []
            for hh in range(2):
                qh = jnp.where(lo if hh == 0 else jnp.logical_not(lo), qb, 0.0).astype(BF16)
                s = lax.dot_general(qh, kb, (((1,), (1,)), ((), ())), preferred_element_type=F32)
                s = jnp.where(valid, s, NEG_BIG)
                m_h = jnp.max(s, axis=-1, keepdims=True)
                p = jnp.exp(s - m_h)
                ls.append(jnp.sum(p, axis=-1, keepdims=True))
                ms.append(m_h)
                os_.append(jnp.dot(p.astype(BF16), vb, preferred_element_type=F32))
            m_b = jnp.where(lo, ms[0], ms[1])
            l_b = jnp.where(lo, ls[0], ls[1])
            o_b = jnp.where(lo, os_[0], os_[1])
            qrows = rows(q0, A_BLOCK, d)
            if d == A_DILATIONS[0]:
                acc[qrows, :] = o_b
                mrun[qrows, :] = m_b
                lrun[qrows, :] = l_b
            else:
                m_o = mrun[qrows, :]
                m_n = jnp.maximum(m_o, m_b)
                a = jnp.exp(m_o - m_n)
                b = jnp.exp(m_b - m_n)
                acc[qrows, :] = a * acc[qrows, :] + b * o_b
                lrun[qrows, :] = a * lrun[qrows, :] + b * l_b
                mrun[qrows, :] = m_n
            return carry

        lax.fori_loop(0, n_blk, body, 0)

    o_ref[...] = (acc[...] / lrun[...]).astype(o_ref.dtype)


def _attention(qa, ka, va, batch, seq):
    t = qa.shape[0]
    nt = seq // A_SPAN
    n_slab = A_WIDTH // LANES
    blk = pl.BlockSpec((A_SPAN, LANES), lambda b, s, i: (b * nt + i, s))
    return pl.pallas_call(
        _attn_kernel,
        grid=(batch, n_slab, nt),
        in_specs=[blk, blk, blk],
        out_specs=blk,
        out_shape=jax.ShapeDtypeStruct((t, A_WIDTH), BF16),
        scratch_shapes=[pltpu.VMEM((2 * A_SPAN, LANES), F32), pltpu.VMEM((2 * A_SPAN, LANES), F32),
                        pltpu.VMEM((A_SPAN, LANES), F32), pltpu.VMEM((A_SPAN, LANES), F32),
                        pltpu.VMEM((A_SPAN, LANES), F32)],
        compiler_params=_cparams("arbitrary", "arbitrary", "arbitrary"),
        name="dilated_attention",
    )(qa, ka, va)


def _retention_tables(chunk):
    log_gamma = jnp.log1p(-jnp.exp2(-5.0 - jnp.arange(R_HEADS, dtype=F32)))
    idx = jnp.arange(chunk, dtype=F32)
    diff = idx[:, None] - idx[None, :]
    decay = jnp.where(diff >= 0, jnp.exp(log_gamma[:, None, None] * jnp.maximum(diff, 0.0)), 0.0)
    zeta = jnp.exp(log_gamma[:, None] * (chunk - 1.0 - idx))
    xi = jnp.exp(log_gamma[:, None] * (idx + 1.0))
    cdec = jnp.exp(log_gamma * chunk)

    def slab(tab):
        tab = tab.reshape(R_HEADS // 2, 2, chunk)
        return jnp.repeat(tab.transpose(0, 2, 1), R_QK_DIM, axis=2)

    cdec_t = jnp.broadcast_to(cdec[:, None, None], (R_HEADS, 1, LANES))
    return decay.astype(F32), slab(zeta).astype(F32), slab(xi).astype(F32), cdec_t.astype(F32)


def _retention_kernel(q_ref, k_ref, v_ref, g_ref, gn_ref, dec_ref, zeta_ref, xi_ref, cdec_ref, o_ref, state,
                      *, chunk):
    @pl.when(pl.program_id(1) == 0)
    def _():
        state[...] = jnp.zeros(state.shape, F32)

    rows_total = q_ref.shape[0]
    lane = lax.broadcasted_iota(jnp.int32, (chunk, LANES), 1)
    lo = lane < R_QK_DIM
    for c in range(rows_total // chunk):
        rs = slice(c * chunk, (c + 1) * chunk)
        for s in range(R_HEADS // 2):
            qs = q_ref[rs, s * LANES:(s + 1) * LANES]
            ks = k_ref[rs, s * LANES:(s + 1) * LANES]
            kz = (ks.astype(F32) * zeta_ref[s]).astype(BF16)
            for hh in range(2):
                h = 2 * s + hh
                mask = lo if hh == 0 else jnp.logical_not(lo)
                qm = jnp.where(mask, qs, jnp.zeros_like(qs))
                vh = v_ref[rs, h * R_V_DIM:(h + 1) * R_V_DIM]
                sc = lax.dot_general(qm, ks, (((1,), (1,)), ((), ())), preferred_element_type=F32)
                sc = (sc * dec_ref[h]).astype(BF16)
                y = jnp.dot(sc, vh, preferred_element_type=F32)
                qx = (qm.astype(F32) * xi_ref[s]).astype(BF16)
                st = state[h]
                y = y + jnp.dot(qx, st.astype(BF16), preferred_element_type=F32)
                kv = lax.dot_general(kz, vh, (((0,), (0,)), ((), ())), preferred_element_type=F32)
                state[h] = cdec_ref[h] * st + kv
                mu = jnp.mean(y, axis=-1, keepdims=True)
                yc = y - mu
                var = jnp.mean(yc * yc, axis=-1, keepdims=True)
                yn = yc * lax.rsqrt(var + 1e-5) * gn_ref[:, h * R_V_DIM:(h + 1) * R_V_DIM]
                g = g_ref[rs, h * R_V_DIM:(h + 1) * R_V_DIM].astype(F32)
                o_ref[rs, h * R_V_DIM:(h + 1) * R_V_DIM] = (g * _sigmoid(g) * yn).astype(o_ref.dtype)


def _retention(qr, kr, vr, gr, gn_g, batch, seq, tr, chunk):
    t = qr.shape[0]
    nt = seq // tr
    decay, zeta, xi, cdec = _retention_tables(chunk)
    row = lambda b, i: (b * nt + i, 0)
    fix2 = lambda b, i: (0, 0)
    fix3 = lambda b, i: (0, 0, 0)
    rqk = R_HEADS * R_QK_DIM
    return pl.pallas_call(
        functools.partial(_retention_kernel, chunk=chunk),
        grid=(batch, nt),
        in_specs=[pl.BlockSpec((tr, rqk), row), pl.BlockSpec((tr, rqk), row),
                  pl.BlockSpec((tr, R_WIDTH), row), pl.BlockSpec((tr, R_WIDTH), row),
                  pl.BlockSpec((1, R_WIDTH), fix2),
                  pl.BlockSpec((R_HEADS, chunk, chunk), fix3),
                  pl.BlockSpec((R_HEADS // 2, chunk, LANES), fix3),
                  pl.BlockSpec((R_HEADS // 2, chunk, LANES), fix3),
                  pl.BlockSpec((R_HEADS, 1, LANES), fix3)],
        out_specs=pl.BlockSpec((tr, R_WIDTH), row),
        out_shape=jax.ShapeDtypeStruct((t, R_WIDTH), BF16),
        scratch_shapes=[pltpu.VMEM((R_HEADS, LANES, R_V_DIM), F32)],
        compiler_params=_cparams("arbitrary", "arbitrary"),
        name="retention",
    )(qr, kr, vr, gr, gn_g.reshape(1, R_WIDTH), decay, zeta, xi, cdec)


def _ssm_matrices(a_re, a_im, b_re, b_im, c_re, c_im, log_dt):
    ell = S_CHUNK
    g_n, p_n, c_n = S_GROUPS, S_STATE, S_GROUP
    dt = jnp.exp(log_dt.astype(F32))[:, None]
    lam_re, lam_im = a_re.astype(F32), a_im.astype(F32)
    mag = jnp.exp(lam_re * dt)
    abar_re = mag * jnp.cos(lam_im * dt)
    abar_im = mag * jnp.sin(lam_im * dt)
    den = lam_re * lam_re + lam_im * lam_im
    nr, ni = abar_re - 1.0, abar_im
    f_re = ((nr * lam_re + ni * lam_im) / den)[..., None]
    f_im = ((ni * lam_re - nr * lam_im) / den)[..., None]
    br, bi = b_re.astype(F32), b_im.astype(F32)
    bb_re = f_re * br - f_im * bi
    bb_im = f_re * bi + f_im * br
    j = jnp.arange(ell + 1, dtype=F32)[:, None, None]
    pw_mag = jnp.exp(j * (lam_re * dt)[None])
    pw_ang = j * (lam_im * dt)[None]
    pw_re = pw_mag * jnp.cos(pw_ang)
    pw_im = pw_mag * jnp.sin(pw_ang)
    cr, ci = c_re.astype(F32), c_im.astype(F32)
    hi = lax.Precision.HIGHEST
    w_re = cr[None] * pw_re[:, :, None, :] - ci[None] * pw_im[:, :, None, :]
    w_im = cr[None] * pw_im[:, :, None, :] + ci[None] * pw_re[:, :, None, :]
    kern = (jnp.einsum('jgcp,gpd->jgcd', w_re[:ell], bb_re, precision=hi)
            - jnp.einsum('jgcp,gpd->jgcd', w_im[:ell], bb_im, precision=hi))
    s_i = np.arange(ell)[:, None]
    t_i = np.arange(ell)[None, :]
    lag = t_i - s_i
    kt = kern[np.clip(lag, 0, None)]
    kt = jnp.where(jnp.asarray(lag >= 0)[:, :, None, None, None], kt, 0.0)
    toep = kt.transpose(2, 0, 4, 1, 3).reshape(g_n, ell * c_n, ell * c_n)
    e_re = pw_re[ell - 1 - np.arange(ell)]
    e_im = pw_im[ell - 1 - np.arange(ell)]
    bz_re = (e_re[:, :, :, None] * bb_re[None] - e_im[:, :, :, None] * bb_im[None])
    bz_im = (e_re[:, :, :, None] * bb_im[None] + e_im[:, :, :, None] * bb_re[None])
    bz_re = bz_re.transpose(1, 0, 3, 2).reshape(g_n, ell * c_n, p_n)
    bz_im = bz_im.transpose(1, 0, 3, 2).reshape(g_n, ell * c_n, p_n)
    zeros = jnp.zeros_like(bz_re[0::2])
    top = jnp.concatenate([bz_re[0::2], zeros, bz_im[0::2], zeros], axis=-1)
    bot = jnp.concatenate([zeros, bz_re[1::2], zeros, bz_im[1::2]], axis=-1)
    bz = jnp.concatenate([top, bot], axis=1)
    cz_re = w_re[1:].transpose(1, 3, 0, 2).reshape(g_n, p_n, ell * c_n)
    cz_im = -w_im[1:].transpose(1, 3, 0, 2).reshape(g_n, p_n, ell * c_n)
    zc = jnp.zeros_like(cz_re[0::2])
    cz = jnp.concatenate([
        jnp.concatenate([cz_re[0::2], zc], axis=-1),
        jnp.concatenate([zc, cz_re[1::2]], axis=-1),
        jnp.concatenate([cz_im[0::2], zc], axis=-1),
        jnp.concatenate([zc, cz_im[1::2]], axis=-1)], axis=1)
    al_re = pw_re[ell].reshape(S_PAIRS, 2 * p_n)
    al_im = pw_im[ell].reshape(S_PAIRS, 2 * p_n)
    a_l = jnp.stack([al_re, al_im], axis=0)
    return toep.astype(BF16), bz.astype(BF16), cz.astype(BF16), a_l.astype(F32)


def _ssm_state_kernel(u_ref, bz_ref, s_ref):
    s_ref[...] = jnp.dot(u_ref[...], bz_ref[0], preferred_element_type=F32)


def _ssm_scan_kernel(s_ref, al_ref, h_ref, *, batch, n_chunks):
    n_blk = 2 * S_PAIRS
    a_re = [al_ref[0, k:k + 1, :] for k in range(S_PAIRS)]
    a_im = [al_ref[1, k:k + 1, :] for k in range(S_PAIRS)]

    def body(n, carry):
        new = []
        for b in range(batch):
            row = b * n_chunks + n
            h = carry[b * n_blk:(b + 1) * n_blk]
            s_row = s_ref[pl.ds(row, 1), :]
            h_ref[pl.ds(row, 1), :] = jnp.concatenate(h, axis=1)
            for k in range(S_PAIRS):
                hr, hi = h[2 * k], h[2 * k + 1]
                sr = s_row[:, (2 * k) * LANES:(2 * k + 1) * LANES]
                si = s_row[:, (2 * k + 1) * LANES:(2 * k + 2) * LANES]
                new.append(a_re[k] * hr - a_im[k] * hi + sr)
                new.append(a_re[k] * hi + a_im[k] * hr + si)
        return tuple(new)

    init = tuple(jnp.zeros((1, LANES), F32) for _ in range(batch * n_blk))
    lax.fori_loop(0, n_chunks, body, init)


def _ssm_out_kernel(u_ref, toep_ref, h_ref, cz_ref, y_ref):
    half = S_CHUNK * S_GROUP
    cross = jnp.dot(h_ref[...].astype(BF16), cz_ref[0], preferred_element_type=F32)
    for g in range(2):
        sl = slice(g * half, (g + 1) * half)
        y_ref[:, sl] = jnp.dot(u_ref[:, sl], toep_ref[g], preferred_element_type=F32) + cross[:, sl]


def _ssm_conv(us, mats, batch, seq):
    toep, bz, cz, a_l = mats
    t = us.shape[0]
    ell = S_CHUNK
    n_chunks = seq // ell
    n_all = batch * n_chunks
    pair_w = 2 * ell * S_GROUP
    u_t = us.astype(BF16).reshape(n_all, ell, S_PAIRS, 2, S_GROUP).transpose(0, 2, 3, 1, 4)
    u_t = u_t.reshape(n_all, S_PAIRS * pair_w)
    st_w = 4 * S_STATE
    s_all = pl.pallas_call(
        _ssm_state_kernel,
        grid=(S_PAIRS,),
        in_specs=[pl.BlockSpec((n_all, pair_w), lambda k: (0, k)),
                  pl.BlockSpec((1, pair_w, st_w), lambda k: (k, 0, 0))],
        out_specs=pl.BlockSpec((n_all, st_w), lambda k: (0, k)),
        out_shape=jax.ShapeDtypeStruct((n_all, S_PAIRS * st_w), F32),
        compiler_params=_cparams("arbitrary"),
        name="ssm_chunk_state",
    )(u_t, bz)
    h_prev = pl.pallas_call(
        functools.partial(_ssm_scan_kernel, batch=batch, n_chunks=n_chunks),
        out_shape=jax.ShapeDtypeStruct((n_all, S_PAIRS * st_w), F32),
        compiler_params=pltpu.CompilerParams(vmem_limit_bytes=VMEM_LIMIT),
        name="ssm_chunk_scan",
    )(s_all, a_l)
    y_t = pl.pallas_call(
        _ssm_out_kernel,
        grid=(S_PAIRS,),
        in_specs=[pl.BlockSpec((n_all, pair_w), lambda k: (0, k)),
                  pl.BlockSpec((2, pair_w // 2, pair_w // 2), lambda k: (k, 0, 0)),
                  pl.BlockSpec((n_all, st_w), lambda k: (0, k)),
                  pl.BlockSpec((1, st_w, pair_w), lambda k: (k, 0, 0))],
        out_specs=pl.BlockSpec((n_all, pair_w), lambda k: (0, k)),
        out_shape=jax.ShapeDtypeStruct((n_all, S_PAIRS * pair_w), F32),
        compiler_params=_cparams("arbitrary"),
        name="ssm_chunk_out",
    )(u_t, toep, h_prev, cz)
    y = y_t.reshape(n_all, S_PAIRS, 2, ell, S_GROUP).transpose(0, 3, 1, 2, 4)
    return y.reshape(t, S_CHANNELS)


def _out_proj_kernel(*refs, with_router):
    if with_router:
        (h_ref, ya_ref, yr_ref, ys_ref, us_ref, d_ref, gw_ref, gb_ref, w_ref, n2_ref, rt_ref,
         h1_ref, xn_ref, lg_ref) = refs
    else:
        (h_ref, ya_ref, yr_ref, ys_ref, us_ref, d_ref, gw_ref, gb_ref, w_ref, n2_ref,
         h1_ref, xn_ref) = refs
    y = ys_ref[...] + d_ref[...] * us_ref[...]
    z = 0.5 * y * (1.0 + jnp.tanh(math.sqrt(2.0 / math.pi) * (y + 0.044715 * (y * y * y))))
    gate = jnp.dot(z.astype(BF16), gw_ref[...], preferred_element_type=F32) + gb_ref[...]
    yc = (z * _sigmoid(gate)).astype(BF16)
    c1 = A_WIDTH
    c2 = A_WIDTH + R_WIDTH
    acc = jnp.dot(ya_ref[...], w_ref[0:c1, :], preferred_element_type=F32)
    acc = acc + jnp.dot(yr_ref[...], w_ref[c1:c2, :], preferred_element_type=F32)
    acc = acc + jnp.dot(yc, w_ref[c2:, :], preferred_element_type=F32)
    h1 = h_ref[...] + acc
    h1_ref[...] = h1
    ms = jnp.mean(h1 * h1, axis=-1, keepdims=True)
    xn = h1 * lax.rsqrt(ms + EPS) * n2_ref[...]
    xn_ref[...] = xn.astype(xn_ref.dtype)
    if with_router:
        lg_ref[...] = jnp.dot(xn, rt_ref[...], preferred_element_type=F32, precision=lax.Precision.HIGHEST)


def _out_proj(h2d, ya, yr, ys, us, d_skip, glu_w, glu_b, w_out, norm2_g, router, tm):
    t, d = h2d.shape
    with_router = router is not None
    row = lambda i: (i, 0)
    fixed = lambda i: (0, 0)
    in_specs = [pl.BlockSpec((tm, d), row), pl.BlockSpec((tm, A_WIDTH), row), pl.BlockSpec((tm, R_WIDTH), row),
                pl.BlockSpec((tm, S_CHANNELS), row), pl.BlockSpec((tm, S_CHANNELS), row),
                pl.BlockSpec((1, S_CHANNELS), fixed), pl.BlockSpec((S_CHANNELS, S_CHANNELS), fixed),
                pl.BlockSpec((1, S_CHANNELS), fixed), pl.BlockSpec(w_out.shape, fixed), pl.BlockSpec((1, d), fixed)]
    args = [h2d, ya, yr, ys, us, d_skip.reshape(1, -1), glu_w, glu_b.reshape(1, -1), w_out, norm2_g.reshape(1, d)]
    out_shapes = [jax.ShapeDtypeStruct((t, d), F32), jax.ShapeDtypeStruct((t, d), F32 if with_router else BF16)]
    out_specs = [pl.BlockSpec((tm, d), row), pl.BlockSpec((tm, d), row)]
    if with_router:
        rt = jnp.zeros((d, LANES), F32).at[:, :router.shape[1]].set(router.astype(F32))
        in_specs.append(pl.BlockSpec((d, LANES), fixed))
        args.append(rt)
        out_shapes.append(jax.ShapeDtypeStruct((t, LANES), F32))
        out_specs.append(pl.BlockSpec((tm, LANES), row))
    return pl.pallas_call(
        functools.partial(_out_proj_kernel, with_router=with_router),
        grid=(t // tm,),
        in_specs=in_specs,
        out_specs=out_specs,
        out_shape=out_shapes,
        compiler_params=_cparams("arbitrary"),
        name="out_proj",
    )(*args)


def _swiglu_chunk(x, wg, wu, wd):
    hg = jnp.dot(x, wg, preferred_element_type=F32)
    hu = jnp.dot(x, wu, preferred_element_type=F32)
    a = (hg * _sigmoid(hg) * hu).astype(BF16)
    return jnp.dot(a, wd, preferred_element_type=F32)


def _ffn_kernel(x_ref, h_ref, wg_ref, wu_ref, wd_ref, o_ref):
    @pl.when(pl.program_id(1) == 0)
    def _():
        o_ref[...] = h_ref[...]

    o_ref[...] += _swiglu_chunk(x_ref[...], wg_ref[...], wu_ref[...], wd_ref[...])


def _ffn(xn, h1, wg, wu, wd, tm, fc):
    t, d = xn.shape
    f = wg.shape[1]
    return pl.pallas_call(
        _ffn_kernel,
        grid=(t // tm, f // fc),
        in_specs=[pl.BlockSpec((tm, d), lambda i, j: (i, 0)), pl.BlockSpec((tm, d), lambda i, j: (i, 0)),
                  pl.BlockSpec((d, fc), lambda i, j: (0, j)), pl.BlockSpec((d, fc), lambda i, j: (0, j)),
                  pl.BlockSpec((fc, d), lambda i, j: (j, 0))],
        out_specs=pl.BlockSpec((tm, d), lambda i, j: (i, 0)),
        out_shape=jax.ShapeDtypeStruct((t, d), F32),
        compiler_params=_cparams("arbitrary", "arbitrary"),
        name="swiglu_ffn",
    )(xn, h1, wg, wu, wd)


def _route_kernel(lg_ref, tri_ref, gate_ref, idx_ref, cnt_ref, carry):
    @pl.when(pl.program_id(0) == 0)
    def _():
        carry[...] = jnp.zeros(carry.shape, F32)

    lg = lg_ref[...]
    lane = lax.broadcasted_iota(jnp.int32, lg.shape, 1)
    valid = lane < N_EXPERTS
    mx = jnp.max(jnp.where(valid, lg, NEG_BIG), axis=-1, keepdims=True)
    ex = jnp.where(valid, jnp.exp(lg - mx), 0.0)
    probs = ex / jnp.sum(ex, axis=-1, keepdims=True)
    p1 = jnp.max(probs, axis=-1, keepdims=True)
    e1 = jnp.min(jnp.where(valid & (probs == p1), lane, LANES), axis=-1, keepdims=True)
    rest = jnp.where(valid & (lane != e1), probs, -1.0)
    p2 = jnp.max(rest, axis=-1, keepdims=True)
    e2 = jnp.min(jnp.where(rest == p2, lane, LANES), axis=-1, keepdims=True)
    den = p1 + p2
    oh1 = lane == e1
    oh2 = lane == e2
    oh = jnp.where(oh1 | oh2, 1.0, 0.0)
    cum = jnp.dot(tri_ref[...], oh.astype(BF16), preferred_element_type=F32)
    excl = cum - oh + carry[...]
    r1 = jnp.sum(jnp.where(oh1, excl, 0.0), axis=-1, keepdims=True)
    r2 = jnp.sum(jnp.where(oh2, excl, 0.0), axis=-1, keepdims=True)
    tot = carry[...] + cum[cum.shape[0] - 1:cum.shape[0], :]
    carry[...] = tot
    cnt_ref[...] = tot.astype(jnp.int32)
    gate_ref[...] = jnp.where(lane == 0, p1 / den, jnp.where(lane == 1, p2 / den, 0.0))
    idx_ref[...] = jnp.where(lane == 0, e1, jnp.where(lane == 1, e2, jnp.where(
        lane == 2, r1.astype(jnp.int32), jnp.where(lane == 3, r2.astype(jnp.int32), 0))))


def _route(logits, tm):
    t = logits.shape[0]
    tri = jnp.asarray(np.tril(np.ones((tm, tm), np.float32)), BF16)
    row = lambda i: (i, 0)
    return pl.pallas_call(
        _route_kernel,
        grid=(t // tm,),
        in_specs=[pl.BlockSpec((tm, LANES), row), pl.BlockSpec((tm, tm), lambda i: (0, 0))],
        out_specs=[pl.BlockSpec((tm, LANES), row), pl.BlockSpec((tm, LANES), row),
                   pl.BlockSpec((1, LANES), lambda i: (0, 0))],
        out_shape=[jax.ShapeDtypeStruct((t, LANES), F32), jax.ShapeDtypeStruct((t, LANES), jnp.int32),
                   jax.ShapeDtypeStruct((1, LANES), jnp.int32)],
        scratch_shapes=[pltpu.VMEM((1, LANES), F32)],
        compiler_params=_cparams("arbitrary"),
        name="moe_route",
    )(logits, tri)


def _dispatch_kernel(pos_ref, fill_ref, x_hbm, zero_hbm, xs_hbm, sem, *, td, tile_rows):
    i = pl.program_id(0)

    def fill_copy(e):
        start = pl.multiple_of(jnp.maximum(fill_ref[e], 0), tile_rows)
        return pltpu.make_async_copy(zero_hbm, xs_hbm.at[pl.ds(start, tile_rows)], sem)

    @pl.when(i == 0)
    def _():
        for e in range(N_EXPERTS):
            @pl.when(fill_ref[e] >= 0)
            def _():
                fill_copy(e).start()
        for e in range(N_EXPERTS):
            @pl.when(fill_ref[e] >= 0)
            def _():
                fill_copy(e).wait()

    def row_copy(tok, slot):
        dst = pos_ref[2 * tok + slot]
        return pltpu.make_async_copy(x_hbm.at[pl.ds(tok, 1)], xs_hbm.at[pl.ds(dst, 1)], sem)

    def issue(j, c):
        tok = i * td + j
        row_copy(tok, 0).start()
        row_copy(tok, 1).start()
        return c

    lax.fori_loop(0, td, issue, 0)

    def drain(j, c):
        tok = i * td + j
        row_copy(tok, 0).wait()
        row_copy(tok, 1).wait()
        return c

    lax.fori_loop(0, td, drain, 0)


def _dispatch(xn, pos_flat, fill_start, p_pad, tile_rows, td):
    t, d = xn.shape
    zeros = jnp.zeros((tile_rows, d), xn.dtype)
    return pl.pallas_call(
        functools.partial(_dispatch_kernel, td=td, tile_rows=tile_rows),
        grid_spec=pltpu.PrefetchScalarGridSpec(
            num_scalar_prefetch=2,
            grid=(t // td,),
            in_specs=[pl.BlockSpec(memory_space=pl.ANY), pl.BlockSpec(memory_space=pl.ANY)],
            out_specs=pl.BlockSpec(memory_space=pl.ANY),
            scratch_shapes=[pltpu.SemaphoreType.DMA(())]),
        out_shape=jax.ShapeDtypeStruct((p_pad, d), xn.dtype),
        compiler_params=pltpu.CompilerParams(dimension_semantics=("arbitrary",)),
        name="moe_dispatch",
    )(pos_flat, fill_start, xn, zeros)


def _moe_kernel(te_ref, nu_ref, x_ref, wg_ref, wu_ref, wd_ref, o_ref, xb):
    i = pl.program_id(0)
    f = pl.program_id(1)

    @pl.when(i < nu_ref[0])
    def _():
        @pl.when(f == 0)
        def _():
            xb[...] = x_ref[...].astype(BF16)
            o_ref[...] = jnp.zeros(o_ref.shape, F32)

        o_ref[...] += _swiglu_chunk(xb[...], wg_ref[0], wu_ref[0], wd_ref[0])


def _moe_experts(xs, tile_expert, n_used, wg, wu, wd, tm, fc):
    p_pad, d = xs.shape
    f = wg.shape[2]
    nf = f // fc
    n_tiles = p_pad // tm

    def x_map(i, j, te, nu):
        return (jnp.minimum(i, nu[0] - 1), 0)

    def f_eff(i, j, nu):
        return jnp.where(i < nu[0], j, nf - 1)

    return pl.pallas_call(
        _moe_kernel,
        grid_spec=pltpu.PrefetchScalarGridSpec(
            num_scalar_prefetch=2,
            grid=(n_tiles, nf),
            in_specs=[pl.BlockSpec((tm, d), x_map),
                      pl.BlockSpec((1, d, fc), lambda i, j, te, nu: (te[i], 0, f_eff(i, j, nu))),
                      pl.BlockSpec((1, d, fc), lambda i, j, te, nu: (te[i], 0, f_eff(i, j, nu))),
                      pl.BlockSpec((1, fc, d), lambda i, j, te, nu: (te[i], f_eff(i, j, nu), 0))],
            out_specs=pl.BlockSpec((tm, d), x_map),
            scratch_shapes=[pltpu.VMEM((tm, d), BF16)]),
        out_shape=jax.ShapeDtypeStruct((p_pad, d), F32),
        compiler_params=_cparams("arbitrary", "arbitrary"),
        name="moe_experts",
    )(tile_expert, n_used, xs, wg, wu, wd)


def _combine_kernel(pos_ref, h_ref, gate_ref, ys_hbm, o_ref, buf, sem, *, tc):
    i = pl.program_id(0)
    n = pl.num_programs(0)

    def row_copy(step, slot_buf, j, k):
        src = pos_ref[2 * (step * tc + j) + k]
        return pltpu.make_async_copy(ys_hbm.at[pl.ds(src, 1)], buf.at[slot_buf, k, pl.ds(j, 1)], sem.at[slot_buf])

    def issue(step, slot_buf):
        def body(j, c):
            row_copy(step, slot_buf, j, 0).start()
            row_copy(step, slot_buf, j, 1).start()
            return c
        lax.fori_loop(0, tc, body, 0)

    @pl.when(i == 0)
    def _():
        issue(0, 0)

    @pl.when(i + 1 < n)
    def _():
        issue(i + 1, (i + 1) % 2)

    cur = i % 2

    def drain(j, c):
        row_copy(i, cur, j, 0).wait()
        row_copy(i, cur, j, 1).wait()
        return c

    lax.fori_loop(0, tc, drain, 0)
    g = gate_ref[...]
    o_ref[...] = h_ref[...] + g[:, 0:1] * buf[cur, 0] + g[:, 1:2] * buf[cur, 1]


def _combine(h1, gates, ys, pos_flat, tc):
    t, d = h1.shape
    row = lambda i, p: (i, 0)
    return pl.pallas_call(
        functools.partial(_combine_kernel, tc=tc),
        grid_spec=pltpu.PrefetchScalarGridSpec(
            num_scalar_prefetch=1,
            grid=(t // tc,),
            in_specs=[pl.BlockSpec((tc, d), row), pl.BlockSpec((tc, LANES), row),
                      pl.BlockSpec(memory_space=pl.ANY)],
            out_specs=pl.BlockSpec((tc, d), row),
            scratch_shapes=[pltpu.VMEM((2, 2, tc, d), F32), pltpu.SemaphoreType.DMA((2,))]),
        out_shape=jax.ShapeDtypeStruct((t, d), F32),
        compiler_params=_cparams("arbitrary"),
        name="moe_combine",
    )(pos_flat, h1, gates, ys)


def _moe(xn, h1, logits, wg, wu, wd, tm, fc):
    t, d = xn.shape
    gates, idx, counts = _route(logits, 512)
    counts = counts[0, :N_EXPERTS]
    tiles_per = (counts + tm - 1) // tm
    tile_end = jnp.cumsum(tiles_per)
    group_start = (tile_end - tiles_per) * tm
    pos = group_start[idx[:, 0:2]] + idx[:, 2:4]
    pos_flat = pos.reshape(-1).astype(jnp.int32)
    n_tiles = (2 * t) // tm + N_EXPERTS
    p_pad = n_tiles * tm
    n_used = tile_end[-1].astype(jnp.int32).reshape(1)
    tile_ids = jnp.minimum(jnp.arange(n_tiles, dtype=jnp.int32), n_used[0] - 1)
    tile_expert = jnp.searchsorted(tile_end, tile_ids, side='right').astype(jnp.int32)
    fill_start = jnp.where(counts > 0, tile_end * tm - tm, -1).astype(jnp.int32)
    xs = _dispatch(xn, pos_flat, fill_start, p_pad, tm, 2048 if t % 2048 == 0 else t)
    ys = _moe_experts(xs, tile_expert, n_used, wg, wu, wd, tm, fc)
    return _combine(h1, gates, ys, pos_flat, 256)


def _trunk(x, norm1_g, w_in, q_norm_g, k_norm_g, ret_gn_g, ssm_a_re, ssm_a_im, ssm_b_re, ssm_b_im,
           ssm_c_re, ssm_c_im, ssm_d, ssm_log_dt, ssm_glu_w, ssm_glu_b, w_out, norm2_g,
           ffn_w_gate, ffn_w_up, ffn_w_down, moe_router, moe_w_gate, moe_w_up, moe_w_down,
           *, tm=512, tr=512, r_chunk=128, ffn_fc=256, moe_tm=512, moe_fc=512):
    batch, seq, d = x.shape
    depth = norm1_g.shape[0]
    t = batch * seq
    tabs = _rope_tables(seq, ROPE_DIM, ROPE_THETA) + _rope_tables(seq, R_QK_DIM, R_ROPE_THETA)
    h = x.reshape(t, d).astype(F32)
    for layer in range(depth):
        qa, ka, va, qr, kr, vr, gr, us = _in_proj(
            h, seq, norm1_g[layer], w_in[layer].astype(BF16), q_norm_g[layer], k_norm_g[layer], tabs, tm)
        ya = _attention(qa, ka, va, batch, seq)
        yr = _retention(qr, kr, vr, gr, ret_gn_g[layer], batch, seq, tr, r_chunk)
        mats = _ssm_matrices(ssm_a_re[layer], ssm_a_im[layer], ssm_b_re[layer], ssm_b_im[layer],
                             ssm_c_re[layer], ssm_c_im[layer], ssm_log_dt[layer])
        ys = _ssm_conv(us, mats, batch, seq)
        i = layer // 2
        router = moe_router[i] if layer % 2 == 1 else None
        outs = _out_proj(h, ya, yr, ys, us, ssm_d[layer], ssm_glu_w[layer].astype(BF16), ssm_glu_b[layer],
                         w_out[layer].astype(BF16), norm2_g[layer], router, tm)
        if layer % 2 == 0:
            h1, xn = outs
            h = _ffn(xn, h1, ffn_w_gate[i].astype(BF16), ffn_w_up[i].astype(BF16), ffn_w_down[i].astype(BF16),
                     2 * tm, ffn_fc)
        else:
            h1, xn, logits = outs
            h = _moe(xn, h1, logits, moe_w_gate[i].astype(BF16), moe_w_up[i].astype(BF16),
                     moe_w_down[i].astype(BF16), moe_tm, moe_fc)
    return h.reshape(batch, seq, d).astype(x.dtype)


def kernel(x, norm1_g, w_in, q_norm_g, k_norm_g, ret_gn_g, ssm_a_re, ssm_a_im, ssm_b_re, ssm_b_im, ssm_c_re,
           ssm_c_im, ssm_d, ssm_log_dt, ssm_glu_w, ssm_glu_b, w_out, norm2_g, ffn_w_gate, ffn_w_up, ffn_w_down,
           moe_router, moe_w_gate, moe_w_up, moe_w_down):
    return _trunk(x, norm1_g, w_in, q_norm_g, k_norm_g, ret_gn_g, ssm_a_re, ssm_a_im, ssm_b_re, ssm_b_im,
                  ssm_c_re, ssm_c_im, ssm_d, ssm_log_dt, ssm_glu_w, ssm_glu_b, w_out, norm2_g,
                  ffn_w_gate, ffn_w_up, ffn_w_down, moe_router, moe_w_gate, moe_w_up, moe_w_down)
```

```python
import functools
import math

import jax
import jax.numpy as jnp
import numpy as np
from jax import lax
from jax.experimental import pallas as pl
from jax.experimental.pallas import tpu as pltpu

F32 = jnp.float32
BF16 = jnp.bfloat16

LANES = 128
EPS = 1e-6
HEAD_DIM = 64
A_HEADS = 4
A_BLOCK = 128
A_DILATIONS = (1, 4, 16)
A_SPAN = A_BLOCK * max(A_DILATIONS)
ROPE_THETA = 500000.0
ROPE_DIM = HEAD_DIM // 4
R_HEADS = 4
R_QK_DIM = 64
R_V_DIM = 128
R_ROPE_THETA = 10000.0
S_CHANNELS = 256
S_GROUP = 16
S_GROUPS = S_CHANNELS // S_GROUP
S_STATE = 64
S_CHUNK = 32
S_PAIRS = S_GROUPS // 2
A_WIDTH = A_HEADS * HEAD_DIM
R_WIDTH = R_HEADS * R_V_DIM
N_EXPERTS = 8
NEG_BIG = -1e30

VMEM_LIMIT = 56 * 1024 * 1024


def _cparams(*sem):
    return pltpu.CompilerParams(dimension_semantics=sem, vmem_limit_bytes=VMEM_LIMIT)


def _sigmoid(x):
    return 1.0 / (1.0 + jnp.exp(-x))


def _rope_tables(seq, rot_dim, theta):
    half = rot_dim // 2
    inv = jnp.power(theta, -jnp.arange(half, dtype=F32) * 2.0 / rot_dim)
    ang = jnp.arange(seq, dtype=jnp.int32).astype(F32)[:, None] * inv[None, :]
    d = np.arange(LANES) % HEAD_DIM
    idx = jnp.asarray(d % half)
    cos = jnp.cos(ang)[:, idx]
    sin = jnp.sin(ang)[:, idx]
    in_rot = jnp.asarray(d < rot_dim)[None, :]
    first = jnp.asarray(d < half)[None, :]
    cos_t = jnp.where(in_rot, cos, 1.0)
    sin_t = jnp.where(in_rot, jnp.where(first, -sin, sin), 0.0)
    return cos_t.astype(F32), sin_t.astype(F32)


def _rope_slab(x, cos_t, sin_t, half):
    lane = lax.broadcasted_iota(jnp.int32, x.shape, 1)
    fwd = pltpu.roll(x, LANES - half, 1)
    bwd = pltpu.roll(x, half, 1)
    partner = jnp.where((lane % (2 * half)) < half, fwd, bwd)
    return x * cos_t + partner * sin_t


def _head_rms_slab(x, g):
    lane = lax.broadcasted_iota(jnp.int32, x.shape, 1)
    lo = lane < HEAD_DIM
    x2 = x * x
    s0 = jnp.sum(jnp.where(lo, x2, 0.0), axis=-1, keepdims=True)
    s1 = jnp.sum(jnp.where(lo, 0.0, x2), axis=-1, keepdims=True)
    ms = jnp.where(lo, s0, s1) * (1.0 / HEAD_DIM)
    return x * lax.rsqrt(ms + EPS) * g


def _in_proj_kernel(x_ref, g_ref, w_ref, qg_ref, kg_ref, ca_ref, sa_ref, cr_ref, sr_ref,
                    qa_ref, ka_ref, va_ref, qr_ref, kr_ref, vr_ref, gr_ref, us_ref):
    x = x_ref[...]
    ms = jnp.mean(x * x, axis=-1, keepdims=True)
    xn = (x * lax.rsqrt(ms + EPS) * g_ref[...]).astype(BF16)

    def proj(c0, n):
        return jnp.dot(xn, w_ref[:, c0:c0 + n], preferred_element_type=F32)

    ca, sa, cr, sr = ca_ref[...], sa_ref[...], cr_ref[...], sr_ref[...]
    qa = proj(0, A_WIDTH)
    ka = proj(A_WIDTH, A_WIDTH)
    for s in range(A_WIDTH // LANES):
        sl = slice(s * LANES, (s + 1) * LANES)
        qn = _rope_slab(_head_rms_slab(qa[:, sl], qg_ref[...]), ca, sa, ROPE_DIM // 2)
        qa_ref[:, sl] = qn * (math.log2(math.e) * HEAD_DIM ** -0.5)
        ka_ref[:, sl] = _rope_slab(_head_rms_slab(ka[:, sl], kg_ref[...]), ca, sa, ROPE_DIM // 2)
    va_ref[...] = proj(2 * A_WIDTH, A_WIDTH)
    c0 = 3 * A_WIDTH
    rqk = R_HEADS * R_QK_DIM
    qr = proj(c0, rqk)
    kr = proj(c0 + rqk, rqk)
    for s in range(rqk // LANES):
        sl = slice(s * LANES, (s + 1) * LANES)
        qr_ref[:, sl] = _rope_slab(qr[:, sl], cr, sr, R_QK_DIM // 2).astype(BF16)
        kr_ref[:, sl] = (_rope_slab(kr[:, sl], cr, sr, R_QK_DIM // 2) * (R_QK_DIM ** -0.5)).astype(BF16)
    c0 += 2 * rqk
    vr_ref[...] = proj(c0, R_WIDTH).astype(BF16)
    gr_ref[...] = proj(c0 + R_WIDTH, R_WIDTH).astype(BF16)
    us_ref[...] = proj(c0 + 2 * R_WIDTH, S_CHANNELS)


def _in_proj(h2d, seq, norm_g, w_in, q_g, k_g, tabs, tm):
    t, d = h2d.shape
    n_cols = w_in.shape[1]
    nt_seq = seq // tm
    row = lambda i: (i, 0)
    fixed = lambda i: (0, 0)
    tab = lambda i: (i % nt_seq, 0)
    out_shapes = (
        jax.ShapeDtypeStruct((t, A_WIDTH), F32), jax.ShapeDtypeStruct((t, A_WIDTH), F32),
        jax.ShapeDtypeStruct((t, A_WIDTH), F32),
        jax.ShapeDtypeStruct((t, R_HEADS * R_QK_DIM), BF16), jax.ShapeDtypeStruct((t, R_HEADS * R_QK_DIM), BF16),
        jax.ShapeDtypeStruct((t, R_WIDTH), BF16), jax.ShapeDtypeStruct((t, R_WIDTH), BF16),
        jax.ShapeDtypeStruct((t, S_CHANNELS), F32))
    return pl.pallas_call(
        _in_proj_kernel,
        grid=(t // tm,),
        in_specs=[pl.BlockSpec((tm, d), row), pl.BlockSpec((1, d), fixed), pl.BlockSpec((d, n_cols), fixed),
                  pl.BlockSpec((1, LANES), fixed), pl.BlockSpec((1, LANES), fixed)]
                 + [pl.BlockSpec((tm, LANES), tab)] * 4,
        out_specs=[pl.BlockSpec((tm, s.shape[1]), row) for s in out_shapes],
        out_shape=out_shapes,
        compiler_params=_cparams("arbitrary"),
        name="in_proj",
    )(h2d, norm_g.reshape(1, d), w_in, jnp.tile(q_g, 2).reshape(1, LANES), jnp.tile(k_g, 2).reshape(1, LANES), *tabs)


def _attn_kernel(q_ref, k_ref, v_ref, o_ref, kbuf, vbuf, acc, mst, lst, bias, *, unroll):
    i = pl.program_id(2)
    span = A_SPAN

    @pl.when(i == 0)
    def _():
        kbuf[0:span, :] = jnp.zeros((span, LANES), F32)
        vbuf[0:span, :] = jnp.zeros((span, LANES), F32)

    @pl.when(i > 0)
    def _():
        kbuf[0:span, :] = kbuf[span:2 * span, :]
        vbuf[0:span, :] = vbuf[span:2 * span, :]

    kbuf[span:2 * span, :] = k_ref[...]
    vbuf[span:2 * span, :] = v_ref[...]

    qi = lax.broadcasted_iota(jnp.int32, (A_BLOCK, 2 * A_BLOCK), 0)
    kj = lax.broadcasted_iota(jnp.int32, (A_BLOCK, 2 * A_BLOCK), 1)
    bias[0] = jnp.where(kj >= qi, jnp.where(kj <= qi + A_BLOCK, 0.0, NEG_BIG), NEG_BIG)
    bias[1] = jnp.where(kj >= jnp.maximum(qi, A_BLOCK), jnp.where(kj <= qi + A_BLOCK, 0.0, NEG_BIG), NEG_BIG)
    lane = lax.broadcasted_iota(jnp.int32, (A_BLOCK, LANES), 1)
    lo = lane < HEAD_DIM

    def rows(start, n, d):
        if d == 1:
            return pl.ds(pl.multiple_of(start, A_BLOCK), n)
        return pl.ds(start, n, stride=d)

    for pi, d in enumerate(A_DILATIONS):
        n_blk = span // A_BLOCK

        def body(blk, carry, d=d, pi=pi):
            if d == 1:
                sp, r = blk, 0
            elif d * A_BLOCK == span:
                sp, r = 0, blk
            else:
                sp, r = blk // d, blk % d
            q0 = sp * (A_BLOCK * d) + r
            qb = q_ref[rows(q0, A_BLOCK, d), :]
            k0 = span + q0 - A_BLOCK * d
            kb = kbuf[rows(k0, 2 * A_BLOCK, d), :].astype(BF16)
            vb = vbuf[rows(k0, 2 * A_BLOCK, d), :].astype(BF16)
            mask = bias[jnp.where(jnp.logical_or(i > 0, sp > 0), 0, 1)]
            ms, ls, os_ = [], [], []
            for hh in range(2):
                qh = jnp.where(lo if hh == 0 else jnp.logical_not(lo), qb, 0.0).astype(BF16)
                s = lax.dot_general(qh, kb, (((1,), (1,)), ((), ())), preferred_element_type=F32) + mask
                m_h = jnp.max(s, axis=-1, keepdims=True)
                p = jnp.exp2(s - m_h)
                ls.append(jnp.sum(p, axis=-1, keepdims=True))
                ms.append(m_h)
                os_.append(jnp.dot(p.astype(BF16), vb, preferred_element_type=F32))
            qrows = rows(q0, A_BLOCK, d)
            acc[pi, qrows, :] = jnp.where(lo, os_[0], os_[1])
            mst[pi, qrows, :] = jnp.where(lo, ms[0], ms[1])
            lst[pi, qrows, :] = jnp.where(lo, ls[0], ls[1])
            return carry

        lax.fori_loop(0, n_blk, body, 0, unroll=unroll)

    n_pat = len(A_DILATIONS)
    cr = 256
    for c in range(span // cr):
        rs = slice(c * cr, (c + 1) * cr)
        m_p = [mst[pi, rs, :] for pi in range(n_pat)]
        m = functools.reduce(jnp.maximum, m_p)
        w = [jnp.exp2(mp - m) for mp in m_p]
        num = sum(w[pi] * acc[pi, rs, :] for pi in range(n_pat))
        den = sum(w[pi] * lst[pi, rs, :] for pi in range(n_pat))
        o_ref[rs, :] = (num / den).astype(o_ref.dtype)


def _attention(qa, ka, va, batch, seq, unroll=8):
    t = qa.shape[0]
    nt = seq // A_SPAN
    n_slab = A_WIDTH // LANES
    n_pat = len(A_DILATIONS)
    blk = pl.BlockSpec((A_SPAN, LANES), lambda b, s, i: (b * nt + i, s))
    return pl.pallas_call(
        functools.partial(_attn_kernel, unroll=unroll),
        grid=(batch, n_slab, nt),
        in_specs=[blk, blk, blk],
        out_specs=blk,
        out_shape=jax.ShapeDtypeStruct((t, A_WIDTH), BF16),
        scratch_shapes=[pltpu.VMEM((2 * A_SPAN, LANES), F32), pltpu.VMEM((2 * A_SPAN, LANES), F32),
                        pltpu.VMEM((n_pat, A_SPAN, LANES), F32), pltpu.VMEM((n_pat, A_SPAN, LANES), F32),
                        pltpu.VMEM((n_pat, A_SPAN, LANES), F32), pltpu.VMEM((2, A_BLOCK, 2 * A_BLOCK), F32)],
        compiler_params=_cparams("arbitrary", "arbitrary", "arbitrary"),
        name="dilated_attention",
    )(qa, ka, va)


def _retention_tables(chunk):
    log_gamma = jnp.log1p(-jnp.exp2(-5.0 - jnp.arange(R_HEADS, dtype=F32)))
    idx = jnp.arange(chunk, dtype=F32)
    diff = idx[:, None] - idx[None, :]
    decay = jnp.where(diff >= 0, jnp.exp(log_gamma[:, None, None] * jnp.maximum(diff, 0.0)), 0.0)
    zeta = jnp.exp(log_gamma[:, None] * (chunk - 1.0 - idx))
    xi = jnp.exp(log_gamma[:, None] * (idx + 1.0))
    cdec = jnp.exp(log_gamma * chunk)

    def slab(tab):
        tab = tab.reshape(R_HEADS // 2, 2, chunk)
        return jnp.repeat(tab.transpose(0, 2, 1), R_QK_DIM, axis=2)

    cdec_t = jnp.broadcast_to(cdec[:, None, None], (R_HEADS, 1, LANES))
    return decay.astype(F32), slab(zeta).astype(F32), slab(xi).astype(F32), cdec_t.astype(F32)


def _retention_kernel(q_ref, k_ref, v_ref, g_ref, gn_ref, dec_ref, zeta_ref, xi_ref, cdec_ref, o_ref, state,
                      *, chunk):
    @pl.when(pl.program_id(1) == 0)
    def _():
        state[...] = jnp.zeros(state.shape, F32)

    rows_total = q_ref.shape[0]
    lane = lax.broadcasted_iota(jnp.int32, (chunk, LANES), 1)
    lo = lane < R_QK_DIM
    for c in range(rows_total // chunk):
        rs = slice(c * chunk, (c + 1) * chunk)
        for s in range(R_HEADS // 2):
            qs = q_ref[rs, s * LANES:(s + 1) * LANES]
            ks = k_ref[rs, s * LANES:(s + 1) * LANES]
            kz = (ks.astype(F32) * zeta_ref[s]).astype(BF16)
            for hh in range(2):
                h = 2 * s + hh
                mask = lo if hh == 0 else jnp.logical_not(lo)
                qm = jnp.where(mask, qs, jnp.zeros_like(qs))
                vh = v_ref[rs, h * R_V_DIM:(h + 1) * R_V_DIM]
                sc = lax.dot_general(qm, ks, (((1,), (1,)), ((), ())), preferred_element_type=F32)
                sc = (sc * dec_ref[h]).astype(BF16)
                y = jnp.dot(sc, vh, preferred_element_type=F32)
                qx = (qm.astype(F32) * xi_ref[s]).astype(BF16)
                st = state[h]
                y = y + jnp.dot(qx, st.astype(BF16), preferred_element_type=F32)
                kv = lax.dot_general(kz, vh, (((0,), (0,)), ((), ())), preferred_element_type=F32)
                state[h] = cdec_ref[h] * st + kv
                mu = jnp.mean(y, axis=-1, keepdims=True)
                yc = y - mu
                var = jnp.mean(yc * yc, axis=-1, keepdims=True)
                yn = yc * lax.rsqrt(var + 1e-5) * gn_ref[:, h * R_V_DIM:(h + 1) * R_V_DIM]
                g = g_ref[rs, h * R_V_DIM:(h + 1) * R_V_DIM].astype(F32)
                o_ref[rs, h * R_V_DIM:(h + 1) * R_V_DIM] = (g * _sigmoid(g) * yn).astype(o_ref.dtype)


def _retention(qr, kr, vr, gr, gn_g, batch, seq, tr, chunk):
    t = qr.shape[0]
    nt = seq // tr
    decay, zeta, xi, cdec = _retention_tables(chunk)
    row = lambda b, i: (b * nt + i, 0)
    fix2 = lambda b, i: (0, 0)
    fix3 = lambda b, i: (0, 0, 0)
    rqk = R_HEADS * R_QK_DIM
    return pl.pallas_call(
        functools.partial(_retention_kernel, chunk=chunk),
        grid=(batch, nt),
        in_specs=[pl.BlockSpec((tr, rqk), row), pl.BlockSpec((tr, rqk), row),
                  pl.BlockSpec((tr, R_WIDTH), row), pl.BlockSpec((tr, R_WIDTH), row),
                  pl.BlockSpec((1, R_WIDTH), fix2),
                  pl.BlockSpec((R_HEADS, chunk, chunk), fix3),
                  pl.BlockSpec((R_HEADS // 2, chunk, LANES), fix3),
                  pl.BlockSpec((R_HEADS // 2, chunk, LANES), fix3),
                  pl.BlockSpec((R_HEADS, 1, LANES), fix3)],
        out_specs=pl.BlockSpec((tr, R_WIDTH), row),
        out_shape=jax.ShapeDtypeStruct((t, R_WIDTH), BF16),
        scratch_shapes=[pltpu.VMEM((R_HEADS, LANES, R_V_DIM), F32)],
        compiler_params=_cparams("arbitrary", "arbitrary"),
        name="retention",
    )(qr, kr, vr, gr, gn_g.reshape(1, R_WIDTH), decay, zeta, xi, cdec)


def _ssm_matrices(a_re, a_im, b_re, b_im, c_re, c_im, log_dt):
    ell = S_CHUNK
    g_n, p_n, c_n = S_GROUPS, S_STATE, S_GROUP
    dt = jnp.exp(log_dt.astype(F32))[:, None]
    lam_re, lam_im = a_re.astype(F32), a_im.astype(F32)
    mag = jnp.exp(lam_re * dt)
    abar_re = mag * jnp.cos(lam_im * dt)
    abar_im = mag * jnp.sin(lam_im * dt)
    den = lam_re * lam_re + lam_im * lam_im
    nr, ni = abar_re - 1.0, abar_im
    f_re = ((nr * lam_re + ni * lam_im) / den)[..., None]
    f_im = ((ni * lam_re - nr * lam_im) / den)[..., None]
    br, bi = b_re.astype(F32), b_im.astype(F32)
    bb_re = f_re * br - f_im * bi
    bb_im = f_re * bi + f_im * br
    j = jnp.arange(ell + 1, dtype=F32)[:, None, None]
    pw_mag = jnp.exp(j * (lam_re * dt)[None])
    pw_ang = j * (lam_im * dt)[None]
    pw_re = pw_mag * jnp.cos(pw_ang)
    pw_im = pw_mag * jnp.sin(pw_ang)
    cr, ci = c_re.astype(F32), c_im.astype(F32)
    hi = lax.Precision.HIGHEST
    w_re = cr[None] * pw_re[:, :, None, :] - ci[None] * pw_im[:, :, None, :]
    w_im = cr[None] * pw_im[:, :, None, :] + ci[None] * pw_re[:, :, None, :]
    kern = (jnp.einsum('jgcp,gpd->jgcd', w_re[:ell], bb_re, precision=hi)
            - jnp.einsum('jgcp,gpd->jgcd', w_im[:ell], bb_im, precision=hi))
    s_i = np.arange(ell)[:, None]
    t_i = np.arange(ell)[None, :]
    lag = t_i - s_i
    kt = kern[np.clip(lag, 0, None)]
    kt = jnp.where(jnp.asarray(lag >= 0)[:, :, None, None, None], kt, 0.0)
    toep = kt.transpose(2, 0, 4, 1, 3).reshape(g_n, ell * c_n, ell * c_n)
    e_re = pw_re[ell - 1 - np.arange(ell)]
    e_im = pw_im[ell - 1 - np.arange(ell)]
    bz_re = (e_re[:, :, :, None] * bb_re[None] - e_im[:, :, :, None] * bb_im[None])
    bz_im = (e_re[:, :, :, None] * bb_im[None] + e_im[:, :, :, None] * bb_re[None])
    bz_re = bz_re.transpose(1, 0, 3, 2).reshape(g_n, ell * c_n, p_n)
    bz_im = bz_im.transpose(1, 0, 3, 2).reshape(g_n, ell * c_n, p_n)
    zeros = jnp.zeros_like(bz_re[0::2])
    top = jnp.concatenate([bz_re[0::2], zeros, bz_im[0::2], zeros], axis=-1)
    bot = jnp.concatenate([zeros, bz_re[1::2], zeros, bz_im[1::2]], axis=-1)
    bz = jnp.concatenate([top, bot], axis=1)
    cz_re = w_re[1:].transpose(1, 3, 0, 2).reshape(g_n, p_n, ell * c_n)
    cz_im = -w_im[1:].transpose(1, 3, 0, 2).reshape(g_n, p_n, ell * c_n)
    zc = jnp.zeros_like(cz_re[0::2])
    cz = jnp.concatenate([
        jnp.concatenate([cz_re[0::2], zc], axis=-1),
        jnp.concatenate([zc, cz_re[1::2]], axis=-1),
        jnp.concatenate([cz_im[0::2], zc], axis=-1),
        jnp.concatenate([zc, cz_im[1::2]], axis=-1)], axis=1)
    al_re = pw_re[ell].reshape(S_PAIRS, 2 * p_n)
    al_im = pw_im[ell].reshape(S_PAIRS, 2 * p_n)
    a_l = jnp.stack([al_re, al_im], axis=0)
    return toep.astype(BF16), bz.astype(BF16), cz.astype(BF16), a_l.astype(F32)


def _ssm_state_kernel(u_ref, bz_ref, s_ref):
    s_ref[...] = jnp.dot(u_ref[...], bz_ref[0], preferred_element_type=F32)


def _ssm_scan_kernel(s_ref, al_ref, h_ref, *, batch, n_chunks):
    n_blk = 2 * S_PAIRS
    a_re = [al_ref[0, k:k + 1, :] for k in range(S_PAIRS)]
    a_im = [al_ref[1, k:k + 1, :] for k in range(S_PAIRS)]

    def body(n, carry):
        new = []
        for b in range(batch):
            row = b * n_chunks + n
            h = carry[b * n_blk:(b + 1) * n_blk]
            s_row = s_ref[pl.ds(row, 1), :]
            h_ref[pl.ds(row, 1), :] = jnp.concatenate(h, axis=1)
            for k in range(S_PAIRS):
                hr, hi = h[2 * k], h[2 * k + 1]
                sr = s_row[:, (2 * k) * LANES:(2 * k + 1) * LANES]
                si = s_row[:, (2 * k + 1) * LANES:(2 * k + 2) * LANES]
                new.append(a_re[k] * hr - a_im[k] * hi + sr)
                new.append(a_re[k] * hi + a_im[k] * hr + si)
        return tuple(new)

    init = tuple(jnp.zeros((1, LANES), F32) for _ in range(batch * n_blk))
    lax.fori_loop(0, n_chunks, body, init)


def _ssm_out_kernel(u_ref, toep_ref, h_ref, cz_ref, y_ref):
    half = S_CHUNK * S_GROUP
    cross = jnp.dot(h_ref[...].astype(BF16), cz_ref[0], preferred_element_type=F32)
    for g in range(2):
        sl = slice(g * half, (g + 1) * half)
        y = jnp.dot(u_ref[:, sl], toep_ref[g], preferred_element_type=F32) + cross[:, sl]
        y_ref[:, sl] = y.astype(y_ref.dtype)


def _ssm_conv(us, mats, batch, seq):
    toep, bz, cz, a_l = mats
    t = us.shape[0]
    ell = S_CHUNK
    n_chunks = seq // ell
    n_all = batch * n_chunks
    pair_w = 2 * ell * S_GROUP
    u_t = us.astype(BF16).reshape(n_all, ell, S_PAIRS, 2, S_GROUP).transpose(0, 2, 3, 1, 4)
    u_t = u_t.reshape(n_all, S_PAIRS * pair_w)
    st_w = 4 * S_STATE
    s_all = pl.pallas_call(
        _ssm_state_kernel,
        grid=(S_PAIRS,),
        in_specs=[pl.BlockSpec((n_all, pair_w), lambda k: (0, k)),
                  pl.BlockSpec((1, pair_w, st_w), lambda k: (k, 0, 0))],
        out_specs=pl.BlockSpec((n_all, st_w), lambda k: (0, k)),
        out_shape=jax.ShapeDtypeStruct((n_all, S_PAIRS * st_w), F32),
        compiler_params=_cparams("arbitrary"),
        name="ssm_chunk_state",
    )(u_t, bz)
    h_prev = pl.pallas_call(
        functools.partial(_ssm_scan_kernel, batch=batch, n_chunks=n_chunks),
        out_shape=jax.ShapeDtypeStruct((n_all, S_PAIRS * st_w), F32),
        compiler_params=pltpu.CompilerParams(vmem_limit_bytes=VMEM_LIMIT),
        name="ssm_chunk_scan",
    )(s_all, a_l)
    y_t = pl.pallas_call(
        _ssm_out_kernel,
        grid=(S_PAIRS,),
        in_specs=[pl.BlockSpec((n_all, pair_w), lambda k: (0, k)),
                  pl.BlockSpec((2, pair_w // 2, pair_w // 2), lambda k: (k, 0, 0)),
                  pl.BlockSpec((n_all, st_w), lambda k: (0, k)),
                  pl.BlockSpec((1, st_w, pair_w), lambda k: (k, 0, 0))],
        out_specs=pl.BlockSpec((n_all, pair_w), lambda k: (0, k)),
        out_shape=jax.ShapeDtypeStruct((n_all, S_PAIRS * pair_w), BF16),
        compiler_params=_cparams("arbitrary"),
        name="ssm_chunk_out",
    )(u_t, toep, h_prev, cz)
    y = y_t.reshape(n_all, S_PAIRS, 2, ell, S_GROUP).transpose(0, 3, 1, 2, 4)
    return y.reshape(t, S_CHANNELS)


def _out_proj_kernel(*refs, with_router):
    if with_router:
        (h_ref, ya_ref, yr_ref, ys_ref, us_ref, d_ref, gw_ref, gb_ref, w_ref, n2_ref, rt_ref,
         h1_ref, xn_ref, lg_ref) = refs
    else:
        (h_ref, ya_ref, yr_ref, ys_ref, us_ref, d_ref, gw_ref, gb_ref, w_ref, n2_ref,
         h1_ref, xn_ref) = refs
    y = ys_ref[...].astype(F32) + d_ref[...] * us_ref[...]
    z = 0.5 * y * (1.0 + jnp.tanh(math.sqrt(2.0 / math.pi) * (y + 0.044715 * (y * y * y))))
    gate = jnp.dot(z.astype(BF16), gw_ref[...], preferred_element_type=F32) + gb_ref[...]
    yc = (z * _sigmoid(gate)).astype(BF16)
    c1 = A_WIDTH
    c2 = A_WIDTH + R_WIDTH
    acc = jnp.dot(ya_ref[...], w_ref[0:c1, :], preferred_element_type=F32)
    acc = acc + jnp.dot(yr_ref[...], w_ref[c1:c2, :], preferred_element_type=F32)
    acc = acc + jnp.dot(yc, w_ref[c2:, :], preferred_element_type=F32)
    h1 = h_ref[...] + acc
    h1_ref[...] = h1
    ms = jnp.mean(h1 * h1, axis=-1, keepdims=True)
    xn = h1 * lax.rsqrt(ms + EPS) * n2_ref[...]
    xn_ref[...] = xn.astype(xn_ref.dtype)
    if with_router:
        x_hi = xn.astype(BF16)
        x_lo = (xn - x_hi.astype(F32)).astype(BF16)
        both = jnp.dot(x_hi, rt_ref[...], preferred_element_type=F32)
        lg_ref[...] = (both[:, :LANES] + both[:, LANES:]
                       + jnp.dot(x_lo, rt_ref[:, :LANES], preferred_element_type=F32))


def _out_proj(h2d, ya, yr, ys, us, d_skip, glu_w, glu_b, w_out, norm2_g, router, tm):
    t, d = h2d.shape
    with_router = router is not None
    row = lambda i: (i, 0)
    fixed = lambda i: (0, 0)
    in_specs = [pl.BlockSpec((tm, d), row), pl.BlockSpec((tm, A_WIDTH), row), pl.BlockSpec((tm, R_WIDTH), row),
                pl.BlockSpec((tm, S_CHANNELS), row), pl.BlockSpec((tm, S_CHANNELS), row),
                pl.BlockSpec((1, S_CHANNELS), fixed), pl.BlockSpec((S_CHANNELS, S_CHANNELS), fixed),
                pl.BlockSpec((1, S_CHANNELS), fixed), pl.BlockSpec(w_out.shape, fixed), pl.BlockSpec((1, d), fixed)]
    args = [h2d, ya, yr, ys, us, d_skip.reshape(1, -1), glu_w, glu_b.reshape(1, -1), w_out, norm2_g.reshape(1, d)]
    out_shapes = [jax.ShapeDtypeStruct((t, d), F32), jax.ShapeDtypeStruct((t, d), F32 if with_router else BF16)]
    out_specs = [pl.BlockSpec((tm, d), row), pl.BlockSpec((tm, d), row)]
    if with_router:
        rt = jnp.pad(router.astype(F32), ((0, 0), (0, LANES - router.shape[1])))
        rt_hi = rt.astype(BF16)
        rt_lo = (rt - rt_hi.astype(F32)).astype(BF16)
        in_specs.append(pl.BlockSpec((d, 2 * LANES), fixed))
        args.append(jnp.concatenate([rt_hi, rt_lo], axis=1))
        out_shapes.append(jax.ShapeDtypeStruct((t, LANES), F32))
        out_specs.append(pl.BlockSpec((tm, LANES), row))
    return pl.pallas_call(
        functools.partial(_out_proj_kernel, with_router=with_router),
        grid=(t // tm,),
        in_specs=in_specs,
        out_specs=out_specs,
        out_shape=out_shapes,
        compiler_params=_cparams("arbitrary"),
        name="out_proj",
    )(*args)


def _swiglu_chunk(x, wg, wu, wd):
    hg = jnp.dot(x, wg, preferred_element_type=F32)
    hu = jnp.dot(x, wu, preferred_element_type=F32)
    a = (hg * _sigmoid(hg) * hu).astype(BF16)
    return jnp.dot(a, wd, preferred_element_type=F32)


def _ffn_kernel(x_ref, h_ref, wg_ref, wu_ref, wd_ref, o_ref):
    @pl.when(pl.program_id(1) == 0)
    def _():
        o_ref[...] = h_ref[...]

    o_ref[...] += _swiglu_chunk(x_ref[...], wg_ref[...], wu_ref[...], wd_ref[...])


def _ffn(xn, h1, wg, wu, wd, tm, fc):
    t, d = xn.shape
    f = wg.shape[1]
    return pl.pallas_call(
        _ffn_kernel,
        grid=(t // tm, f // fc),
        in_specs=[pl.BlockSpec((tm, d), lambda i, j: (i, 0)), pl.BlockSpec((tm, d), lambda i, j: (i, 0)),
                  pl.BlockSpec((d, fc), lambda i, j: (0, j)), pl.BlockSpec((d, fc), lambda i, j: (0, j)),
                  pl.BlockSpec((fc, d), lambda i, j: (j, 0))],
        out_specs=pl.BlockSpec((tm, d), lambda i, j: (i, 0)),
        out_shape=jax.ShapeDtypeStruct((t, d), F32),
        compiler_params=_cparams("arbitrary", "arbitrary"),
        name="swiglu_ffn",
    )(xn, h1, wg, wu, wd)


def _route_kernel(lg_ref, tri_ref, gate_ref, idx_ref, cnt_ref, carry):
    @pl.when(pl.program_id(0) == 0)
    def _():
        carry[...] = jnp.zeros(carry.shape, F32)

    lg = lg_ref[...]
    lane = lax.broadcasted_iota(jnp.int32, lg.shape, 1)
    valid = lane < N_EXPERTS
    mx = jnp.max(jnp.where(valid, lg, NEG_BIG), axis=-1, keepdims=True)
    ex = jnp.where(valid, jnp.exp(lg - mx), 0.0)
    probs = ex / jnp.sum(ex, axis=-1, keepdims=True)
    p1 = jnp.max(probs, axis=-1, keepdims=True)
    e1 = jnp.min(jnp.where(valid & (probs == p1), lane, LANES), axis=-1, keepdims=True)
    rest = jnp.where(valid & (lane != e1), probs, -1.0)
    p2 = jnp.max(rest, axis=-1, keepdims=True)
    e2 = jnp.min(jnp.where(rest == p2, lane, LANES), axis=-1, keepdims=True)
    den = p1 + p2
    oh1 = lane == e1
    oh2 = lane == e2
    oh = jnp.where(oh1 | oh2, 1.0, 0.0)
    cum = jnp.dot(tri_ref[...], oh.astype(BF16), preferred_element_type=F32)
    excl = cum - oh + carry[...]
    r1 = jnp.sum(jnp.where(oh1, excl, 0.0), axis=-1, keepdims=True)
    r2 = jnp.sum(jnp.where(oh2, excl, 0.0), axis=-1, keepdims=True)
    tot = carry[...] + cum[cum.shape[0] - 1:cum.shape[0], :]
    carry[...] = tot
    cnt_ref[...] = tot.astype(jnp.int32)
    gate_ref[...] = jnp.where(lane == 0, p1 / den, jnp.where(lane == 1, p2 / den, 0.0))
    idx_ref[...] = jnp.where(lane == 0, e1, jnp.where(lane == 1, e2, jnp.where(
        lane == 2, r1.astype(jnp.int32), jnp.where(lane == 3, r2.astype(jnp.int32), 0))))


def _route(logits, tm):
    t = logits.shape[0]
    tri = jnp.asarray(np.tril(np.ones((tm, tm), np.float32)), BF16)
    row = lambda i: (i, 0)
    return pl.pallas_call(
        _route_kernel,
        grid=(t // tm,),
        in_specs=[pl.BlockSpec((tm, LANES), row), pl.BlockSpec((tm, tm), lambda i: (0, 0))],
        out_specs=[pl.BlockSpec((tm, LANES), row), pl.BlockSpec((tm, LANES), row),
                   pl.BlockSpec((1, LANES), lambda i: (0, 0))],
        out_shape=[jax.ShapeDtypeStruct((t, LANES), F32), jax.ShapeDtypeStruct((t, LANES), jnp.int32),
                   jax.ShapeDtypeStruct((1, LANES), jnp.int32)],
        scratch_shapes=[pltpu.VMEM((1, LANES), F32)],
        compiler_params=_cparams("arbitrary"),
        name="moe_route",
    )(logits, tri)


DMA_UNROLL = 8


def _dispatch_kernel(pos_ref, fill_ref, x_ref, zero_hbm, xs_hbm, sem, *, td, tile_rows):
    i = pl.program_id(0)

    def fill_copy(e):
        start = pl.multiple_of(jnp.maximum(fill_ref[e], 0), tile_rows)
        return pltpu.make_async_copy(zero_hbm, xs_hbm.at[pl.ds(start, tile_rows)], sem)

    @pl.when(i == 0)
    def _():
        for e in range(N_EXPERTS):
            @pl.when(fill_ref[e] >= 0)
            def _():
                fill_copy(e).start()
        for e in range(N_EXPERTS):
            @pl.when(fill_ref[e] >= 0)
            def _():
                fill_copy(e).wait()

    def issue(jb, c):
        for u in range(DMA_UNROLL):
            j = jb * DMA_UNROLL + u
            for slot in range(2):
                dst = pos_ref[2 * (i * td + j) + slot]
                pltpu.make_async_copy(x_ref.at[pl.ds(j, 1)], xs_hbm.at[pl.ds(dst, 1)], sem).start(priority=slot)
        return c

    lax.fori_loop(0, td // DMA_UNROLL, issue, 0)
    for slot in range(2):
        pltpu.make_async_copy(x_ref, xs_hbm.at[pl.ds(0, td)], sem).wait()


def _dispatch(xn, pos_flat, fill_start, p_pad, tile_rows, td):
    t, d = xn.shape
    zeros = jnp.zeros((tile_rows, d), xn.dtype)
    return pl.pallas_call(
        functools.partial(_dispatch_kernel, td=td, tile_rows=tile_rows),
        grid_spec=pltpu.PrefetchScalarGridSpec(
            num_scalar_prefetch=2,
            grid=(t // td,),
            in_specs=[pl.BlockSpec((td, d), lambda i, p, f: (i, 0)), pl.BlockSpec(memory_space=pl.ANY)],
            out_specs=pl.BlockSpec(memory_space=pl.ANY),
            scratch_shapes=[pltpu.SemaphoreType.DMA(())]),
        out_shape=jax.ShapeDtypeStruct((p_pad, d), xn.dtype),
        compiler_params=pltpu.CompilerParams(dimension_semantics=("arbitrary",)),
        name="moe_dispatch",
    )(pos_flat, fill_start, xn, zeros)


def _moe_kernel(te_ref, nu_ref, x_ref, wg_ref, wu_ref, wd_ref, o_ref, xb):
    i = pl.program_id(0)
    f = pl.program_id(1)

    @pl.when(i < nu_ref[0])
    def _():
        @pl.when(f == 0)
        def _():
            xb[...] = x_ref[...].astype(BF16)
            o_ref[...] = jnp.zeros(o_ref.shape, F32)

        o_ref[...] += _swiglu_chunk(xb[...], wg_ref[0], wu_ref[0], wd_ref[0])


def _moe_experts(xs, tile_expert, n_used, wg, wu, wd, tm, fc):
    p_pad, d = xs.shape
    f = wg.shape[2]
    nf = f // fc
    n_tiles = p_pad // tm

    def x_map(i, j, te, nu):
        return (jnp.minimum(i, nu[0] - 1), 0)

    def f_eff(i, j, nu):
        return jnp.where(i < nu[0], j, nf - 1)

    return pl.pallas_call(
        _moe_kernel,
        grid_spec=pltpu.PrefetchScalarGridSpec(
            num_scalar_prefetch=2,
            grid=(n_tiles, nf),
            in_specs=[pl.BlockSpec((tm, d), x_map),
                      pl.BlockSpec((1, d, fc), lambda i, j, te, nu: (te[i], 0, f_eff(i, j, nu))),
                      pl.BlockSpec((1, d, fc), lambda i, j, te, nu: (te[i], 0, f_eff(i, j, nu))),
                      pl.BlockSpec((1, fc, d), lambda i, j, te, nu: (te[i], f_eff(i, j, nu), 0))],
            out_specs=pl.BlockSpec((tm, d), x_map),
            scratch_shapes=[pltpu.VMEM((tm, d), BF16)]),
        out_shape=jax.ShapeDtypeStruct((p_pad, d), F32),
        compiler_params=_cparams("arbitrary", "arbitrary"),
        name="moe_experts",
    )(tile_expert, n_used, xs, wg, wu, wd)


def _combine_kernel(pos_ref, h_ref, gate_ref, ys_hbm, o_ref, buf, sem, *, tc):
    i = pl.program_id(0)
    n = pl.num_programs(0)

    def issue(step, slot_buf):
        def body(jb, c):
            for u in range(DMA_UNROLL):
                j = jb * DMA_UNROLL + u
                for k in range(2):
                    src = pos_ref[2 * (step * tc + j) + k]
                    pltpu.make_async_copy(ys_hbm.at[pl.ds(src, 1)], buf.at[slot_buf, k, pl.ds(j, 1)],
                                          sem.at[slot_buf]).start(priority=k)
            return c
        lax.fori_loop(0, tc // DMA_UNROLL, body, 0)

    @pl.when(i == 0)
    def _():
        issue(0, 0)

    @pl.when(i + 1 < n)
    def _():
        issue(i + 1, (i + 1) % 2)

    cur = i % 2
    for k in range(2):
        pltpu.make_async_copy(ys_hbm.at[pl.ds(0, tc)], buf.at[cur, k], sem.at[cur]).wait()
    g = gate_ref[...]
    o_ref[...] = h_ref[...] + g[:, 0:1] * buf[cur, 0] + g[:, 1:2] * buf[cur, 1]


def _combine(h1, gates, ys, pos_flat, tc):
    t, d = h1.shape
    row = lambda i, p: (i, 0)
    return pl.pallas_call(
        functools.partial(_combine_kernel, tc=tc),
        grid_spec=pltpu.PrefetchScalarGridSpec(
            num_scalar_prefetch=1,
            grid=(t // tc,),
            in_specs=[pl.BlockSpec((tc, d), row), pl.BlockSpec((tc, LANES), row),
                      pl.BlockSpec(memory_space=pl.ANY)],
            out_specs=pl.BlockSpec((tc, d), row),
            scratch_shapes=[pltpu.VMEM((2, 2, tc, d), F32), pltpu.SemaphoreType.DMA((2,))]),
        out_shape=jax.ShapeDtypeStruct((t, d), F32),
        compiler_params=_cparams("arbitrary"),
        name="moe_combine",
    )(pos_flat, h1, gates, ys)


def _moe(xn, h1, logits, wg, wu, wd, tm, fc):
    t, d = xn.shape
    gates, idx, counts = _route(logits, 512)
    counts = counts[0, :N_EXPERTS]
    tiles_per = (counts + tm - 1) // tm
    tile_end = jnp.cumsum(tiles_per)
    group_start = (tile_end - tiles_per) * tm
    pos = group_start[idx[:, 0:2]] + idx[:, 2:4]
    pos_flat = pos.reshape(-1).astype(jnp.int32)
    n_tiles = (2 * t) // tm + N_EXPERTS
    p_pad = n_tiles * tm
    n_used = tile_end[-1].astype(jnp.int32).reshape(1)
    tile_ids = jnp.minimum(jnp.arange(n_tiles, dtype=jnp.int32), n_used[0] - 1)
    tile_expert = jnp.sum(tile_ids[:, None] >= tile_end[None, :], axis=1).astype(jnp.int32)
    fill_start = jnp.where(counts > 0, tile_end * tm - tm, -1).astype(jnp.int32)
    xs = _dispatch(xn, pos_flat, fill_start, p_pad, tm, 512)
    ys = _moe_experts(xs, tile_expert, n_used, wg, wu, wd, tm, fc)
    return _combine(h1, gates, ys, pos_flat, 256)


def _trunk(x, norm1_g, w_in, q_norm_g, k_norm_g, ret_gn_g, ssm_a_re, ssm_a_im, ssm_b_re, ssm_b_im,
           ssm_c_re, ssm_c_im, ssm_d, ssm_log_dt, ssm_glu_w, ssm_glu_b, w_out, norm2_g,
           ffn_w_gate, ffn_w_up, ffn_w_down, moe_router, moe_w_gate, moe_w_up, moe_w_down,
           *, tm=512, tr=512, r_chunk=128, ffn_fc=256, moe_tm=512, moe_fc=512):
    batch, seq, d = x.shape
    depth = norm1_g.shape[0]
    t = batch * seq
    tabs = _rope_tables(seq, ROPE_DIM, ROPE_THETA) + _rope_tables(seq, R_QK_DIM, R_ROPE_THETA)
    h = x.reshape(t, d).astype(F32)
    for layer in range(depth):
        qa, ka, va, qr, kr, vr, gr, us = _in_proj(
            h, seq, norm1_g[layer], w_in[layer].astype(BF16), q_norm_g[layer], k_norm_g[layer], tabs, tm)
        ya = _attention(qa, ka, va, batch, seq)
        yr = _retention(qr, kr, vr, gr, ret_gn_g[layer], batch, seq, tr, r_chunk)
        mats = _ssm_matrices(ssm_a_re[layer], ssm_a_im[layer], ssm_b_re[layer], ssm_b_im[layer],
                             ssm_c_re[layer], ssm_c_im[layer], ssm_log_dt[layer])
        ys = _ssm_conv(us, mats, batch, seq)
        i = layer // 2
        router = moe_router[i] if layer % 2 == 1 else None
        outs = _out_proj(h, ya, yr, ys, us, ssm_d[layer], ssm_glu_w[layer].astype(BF16), ssm_glu_b[layer],
                         w_out[layer].astype(BF16), norm2_g[layer], router, tm)
        if layer % 2 == 0:
            h1, xn = outs
            h = _ffn(xn, h1, ffn_w_gate[i].astype(BF16), ffn_w_up[i].astype(BF16), ffn_w_down[i].astype(BF16),
                     2 * tm, ffn_fc)
        else:
            h1, xn, logits = outs
            h = _moe(xn, h1, logits, moe_w_gate[i].astype(BF16), moe_w_up[i].astype(BF16),
                     moe_w_down[i].astype(BF16), moe_tm, moe_fc)
    return h.reshape(batch, seq, d).astype(x.dtype)


def kernel(x, norm1_g, w_in, q_norm_g, k_norm_g, ret_gn_g, ssm_a_re, ssm_a_im, ssm_b_re, ssm_b_im, ssm_c_re,
           ssm_c_im, ssm_d, ssm_log_dt, ssm_glu_w, ssm_glu_b, w_out, norm2_g, ffn_w_gate, ffn_w_up, ffn_w_down,
           moe_router, moe_w_gate, moe_w_up, moe_w_down):
    return _trunk(x, norm1_g, w_in, q_norm_g, k_norm_g, ret_gn_g, ssm_a_re, ssm_a_im, ssm_b_re, ssm_b_im,
                  ssm_c_re, ssm_c_im, ssm_d, ssm_log_dt, ssm_glu_w, ssm_glu_b, w_out, norm2_g,
                  ffn_w_gate, ffn_w_up, ffn_w_down, moe_router, moe_w_gate, moe_w_up, moe_w_down)
```

```python
import functools
import math

import jax
import jax.numpy as jnp
import numpy as np
from jax import lax
from jax.experimental import pallas as pl
from jax.experimental.pallas import tpu as pltpu

F32 = jnp.float32
BF16 = jnp.bfloat16

LANES = 128
EPS = 1e-6
HEAD_DIM = 64
A_HEADS = 4
A_BLOCK = 128
A_DILATIONS = (1, 4, 16)
A_SPAN = A_BLOCK * max(A_DILATIONS)
ROPE_THETA = 500000.0
ROPE_DIM = HEAD_DIM // 4
R_HEADS = 4
R_QK_DIM = 64
R_V_DIM = 128
R_ROPE_THETA = 10000.0
S_CHANNELS = 256
S_GROUP = 16
S_GROUPS = S_CHANNELS // S_GROUP
S_STATE = 64
S_CHUNK = 32
S_PAIRS = S_GROUPS // 2
A_WIDTH = A_HEADS * HEAD_DIM
R_WIDTH = R_HEADS * R_V_DIM
N_EXPERTS = 8
NEG_BIG = -1e30

VMEM_LIMIT = 56 * 1024 * 1024


def _cparams(*sem):
    return pltpu.CompilerParams(dimension_semantics=sem, vmem_limit_bytes=VMEM_LIMIT)


def _sigmoid(x):
    return 1.0 / (1.0 + jnp.exp(-x))


def _rope_tables(seq, rot_dim, theta):
    half = rot_dim // 2
    inv = jnp.power(theta, -jnp.arange(half, dtype=F32) * 2.0 / rot_dim)
    ang = jnp.arange(seq, dtype=jnp.int32).astype(F32)[:, None] * inv[None, :]
    d = np.arange(LANES) % HEAD_DIM
    idx = jnp.asarray(d % half)
    cos = jnp.cos(ang)[:, idx]
    sin = jnp.sin(ang)[:, idx]
    in_rot = jnp.asarray(d < rot_dim)[None, :]
    first = jnp.asarray(d < half)[None, :]
    cos_t = jnp.where(in_rot, cos, 1.0)
    sin_t = jnp.where(in_rot, jnp.where(first, -sin, sin), 0.0)
    return cos_t.astype(F32), sin_t.astype(F32)


def _rope_slab(x, cos_t, sin_t, half):
    lane = lax.broadcasted_iota(jnp.int32, x.shape, 1)
    fwd = pltpu.roll(x, LANES - half, 1)
    bwd = pltpu.roll(x, half, 1)
    partner = jnp.where((lane % (2 * half)) < half, fwd, bwd)
    return x * cos_t + partner * sin_t


def _head_rms_slab(x, g):
    lane = lax.broadcasted_iota(jnp.int32, x.shape, 1)
    lo = lane < HEAD_DIM
    x2 = x * x
    s0 = jnp.sum(jnp.where(lo, x2, 0.0), axis=-1, keepdims=True)
    s1 = jnp.sum(jnp.where(lo, 0.0, x2), axis=-1, keepdims=True)
    ms = jnp.where(lo, s0, s1) * (1.0 / HEAD_DIM)
    return x * lax.rsqrt(ms + EPS) * g


def _in_proj_kernel(x_ref, g_ref, w_ref, qg_ref, kg_ref, ca_ref, sa_ref, cr_ref, sr_ref,
                    qa_ref, ka_ref, va_ref, qr_ref, kr_ref, vr_ref, gr_ref, us_ref):
    x = x_ref[...]
    ms = jnp.mean(x * x, axis=-1, keepdims=True)
    xn = (x * lax.rsqrt(ms + EPS) * g_ref[...]).astype(BF16)

    def proj(c0, n):
        return jnp.dot(xn, w_ref[:, c0:c0 + n], preferred_element_type=F32)

    ca, sa, cr, sr = ca_ref[...], sa_ref[...], cr_ref[...], sr_ref[...]
    qa = proj(0, A_WIDTH)
    ka = proj(A_WIDTH, A_WIDTH)
    for s in range(A_WIDTH // LANES):
        sl = slice(s * LANES, (s + 1) * LANES)
        qn = _rope_slab(_head_rms_slab(qa[:, sl], qg_ref[...]), ca, sa, ROPE_DIM // 2)
        qa_ref[:, sl] = qn * (math.log2(math.e) * HEAD_DIM ** -0.5)
        ka_ref[:, sl] = _rope_slab(_head_rms_slab(ka[:, sl], kg_ref[...]), ca, sa, ROPE_DIM // 2)
    va_ref[...] = proj(2 * A_WIDTH, A_WIDTH)
    c0 = 3 * A_WIDTH
    rqk = R_HEADS * R_QK_DIM
    qr = proj(c0, rqk)
    kr = proj(c0 + rqk, rqk)
    for s in range(rqk // LANES):
        sl = slice(s * LANES, (s + 1) * LANES)
        qr_ref[:, sl] = _rope_slab(qr[:, sl], cr, sr, R_QK_DIM // 2).astype(BF16)
        kr_ref[:, sl] = (_rope_slab(kr[:, sl], cr, sr, R_QK_DIM // 2) * (R_QK_DIM ** -0.5)).astype(BF16)
    c0 += 2 * rqk
    vr_ref[...] = proj(c0, R_WIDTH).astype(BF16)
    gr_ref[...] = proj(c0 + R_WIDTH, R_WIDTH).astype(BF16)
    us_ref[...] = proj(c0 + 2 * R_WIDTH, S_CHANNELS)


def _in_proj(h2d, seq, norm_g, w_in, q_g, k_g, tabs, tm):
    t, d = h2d.shape
    n_cols = w_in.shape[1]
    nt_seq = seq // tm
    row = lambda i: (i, 0)
    fixed = lambda i: (0, 0)
    tab = lambda i: (i % nt_seq, 0)
    out_shapes = (
        jax.ShapeDtypeStruct((t, A_WIDTH), F32), jax.ShapeDtypeStruct((t, A_WIDTH), F32),
        jax.ShapeDtypeStruct((t, A_WIDTH), F32),
        jax.ShapeDtypeStruct((t, R_HEADS * R_QK_DIM), BF16), jax.ShapeDtypeStruct((t, R_HEADS * R_QK_DIM), BF16),
        jax.ShapeDtypeStruct((t, R_WIDTH), BF16), jax.ShapeDtypeStruct((t, R_WIDTH), BF16),
        jax.ShapeDtypeStruct((t, S_CHANNELS), F32))
    return pl.pallas_call(
        _in_proj_kernel,
        grid=(t // tm,),
        in_specs=[pl.BlockSpec((tm, d), row), pl.BlockSpec((1, d), fixed), pl.BlockSpec((d, n_cols), fixed),
                  pl.BlockSpec((1, LANES), fixed), pl.BlockSpec((1, LANES), fixed)]
                 + [pl.BlockSpec((tm, LANES), tab)] * 4,
        out_specs=[pl.BlockSpec((tm, s.shape[1]), row) for s in out_shapes],
        out_shape=out_shapes,
        compiler_params=_cparams("arbitrary"),
        name="in_proj",
    )(h2d, norm_g.reshape(1, d), w_in, jnp.tile(q_g, 2).reshape(1, LANES), jnp.tile(k_g, 2).reshape(1, LANES), *tabs)


def _attn_kernel(q_ref, k_ref, v_ref, o_ref, kbuf, vbuf, acc, mst, lst, bias, *, unroll):
    i = pl.program_id(2)
    span = A_SPAN

    @pl.when(i == 0)
    def _():
        kbuf[0:span, :] = jnp.zeros((span, LANES), F32)
        vbuf[0:span, :] = jnp.zeros((span, LANES), F32)

    @pl.when(i > 0)
    def _():
        kbuf[0:span, :] = kbuf[span:2 * span, :]
        vbuf[0:span, :] = vbuf[span:2 * span, :]

    kbuf[span:2 * span, :] = k_ref[...]
    vbuf[span:2 * span, :] = v_ref[...]

    qi = lax.broadcasted_iota(jnp.int32, (A_BLOCK, 2 * A_BLOCK), 0)
    kj = lax.broadcasted_iota(jnp.int32, (A_BLOCK, 2 * A_BLOCK), 1)
    bias[0] = jnp.where(kj >= qi, jnp.where(kj <= qi + A_BLOCK, 0.0, NEG_BIG), NEG_BIG)
    bias[1] = jnp.where(kj >= jnp.maximum(qi, A_BLOCK), jnp.where(kj <= qi + A_BLOCK, 0.0, NEG_BIG), NEG_BIG)
    lane = lax.broadcasted_iota(jnp.int32, (A_BLOCK, LANES), 1)
    lo = lane < HEAD_DIM

    def rows(start, n, d):
        if d == 1:
            return pl.ds(pl.multiple_of(start, A_BLOCK), n)
        return pl.ds(start, n, stride=d)

    for pi, d in enumerate(A_DILATIONS):
        n_blk = span // A_BLOCK

        def body(blk, carry, d=d, pi=pi):
            if d == 1:
                sp, r = blk, 0
            elif d * A_BLOCK == span:
                sp, r = 0, blk
            else:
                sp, r = blk // d, blk % d
            q0 = sp * (A_BLOCK * d) + r
            qb = q_ref[rows(q0, A_BLOCK, d), :]
            k0 = span + q0 - A_BLOCK * d
            kb = kbuf[rows(k0, 2 * A_BLOCK, d), :].astype(BF16)
            vb = vbuf[rows(k0, 2 * A_BLOCK, d), :].astype(BF16)
            mask = bias[jnp.where(jnp.logical_or(i > 0, sp > 0), 0, 1)]
            ms, ls, os_ = [], [], []
            for hh in range(2):
                qh = jnp.where(lo if hh == 0 else jnp.logical_not(lo), qb, 0.0).astype(BF16)
                s = lax.dot_general(qh, kb, (((1,), (1,)), ((), ())), preferred_element_type=F32) + mask
                m_h = jnp.max(s, axis=-1, keepdims=True)
                p = jnp.exp2(s - m_h)
                ls.append(jnp.sum(p, axis=-1, keepdims=True))
                ms.append(m_h)
                os_.append(jnp.dot(p.astype(BF16), vb, preferred_element_type=F32))
            qrows = rows(q0, A_BLOCK, d)
            acc[pi, qrows, :] = jnp.where(lo, os_[0], os_[1])
            mst[pi, qrows, :] = jnp.where(lo, ms[0], ms[1])
            lst[pi, qrows, :] = jnp.where(lo, ls[0], ls[1])
            return carry

        lax.fori_loop(0, n_blk, body, 0, unroll=unroll)

    n_pat = len(A_DILATIONS)
    cr = 256
    for c in range(span // cr):
        rs = slice(c * cr, (c + 1) * cr)
        m_p = [mst[pi, rs, :] for pi in range(n_pat)]
        m = functools.reduce(jnp.maximum, m_p)
        w = [jnp.exp2(mp - m) for mp in m_p]
        num = sum(w[pi] * acc[pi, rs, :] for pi in range(n_pat))
        den = sum(w[pi] * lst[pi, rs, :] for pi in range(n_pat))
        o_ref[rs, :] = (num / den).astype(o_ref.dtype)


def _attention(qa, ka, va, batch, seq, unroll=8):
    t = qa.shape[0]
    nt = seq // A_SPAN
    n_slab = A_WIDTH // LANES
    n_pat = len(A_DILATIONS)
    blk = pl.BlockSpec((A_SPAN, LANES), lambda b, s, i: (b * nt + i, s))
    return pl.pallas_call(
        functools.partial(_attn_kernel, unroll=unroll),
        grid=(batch, n_slab, nt),
        in_specs=[blk, blk, blk],
        out_specs=blk,
        out_shape=jax.ShapeDtypeStruct((t, A_WIDTH), BF16),
        scratch_shapes=[pltpu.VMEM((2 * A_SPAN, LANES), F32), pltpu.VMEM((2 * A_SPAN, LANES), F32),
                        pltpu.VMEM((n_pat, A_SPAN, LANES), F32), pltpu.VMEM((n_pat, A_SPAN, LANES), F32),
                        pltpu.VMEM((n_pat, A_SPAN, LANES), F32), pltpu.VMEM((2, A_BLOCK, 2 * A_BLOCK), F32)],
        compiler_params=_cparams("arbitrary", "arbitrary", "arbitrary"),
        name="dilated_attention",
    )(qa, ka, va)


def _retention_tables(chunk):
    log_gamma = jnp.log1p(-jnp.exp2(-5.0 - jnp.arange(R_HEADS, dtype=F32)))
    idx = jnp.arange(chunk, dtype=F32)
    diff = idx[:, None] - idx[None, :]
    decay = jnp.where(diff >= 0, jnp.exp(log_gamma[:, None, None] * jnp.maximum(diff, 0.0)), 0.0)
    zeta = jnp.exp(log_gamma[:, None] * (chunk - 1.0 - idx))
    xi = jnp.exp(log_gamma[:, None] * (idx + 1.0))
    cdec = jnp.exp(log_gamma * chunk)

    def slab(tab):
        tab = tab.reshape(R_HEADS // 2, 2, chunk)
        return jnp.repeat(tab.transpose(0, 2, 1), R_QK_DIM, axis=2)

    cdec_t = jnp.broadcast_to(cdec[:, None, None], (R_HEADS, 1, LANES))
    return decay.astype(F32), slab(zeta).astype(F32), slab(xi).astype(F32), cdec_t.astype(F32)


def _retention_kernel(q_ref, k_ref, v_ref, g_ref, gn_ref, dec_ref, zeta_ref, xi_ref, cdec_ref, o_ref, state,
                      *, chunk):
    @pl.when(pl.program_id(1) == 0)
    def _():
        state[...] = jnp.zeros(state.shape, F32)

    rows_total = q_ref.shape[0]
    lane = lax.broadcasted_iota(jnp.int32, (chunk, LANES), 1)
    lo = lane < R_QK_DIM
    for c in range(rows_total // chunk):
        rs = slice(c * chunk, (c + 1) * chunk)
        for s in range(R_HEADS // 2):
            qs = q_ref[rs, s * LANES:(s + 1) * LANES]
            ks = k_ref[rs, s * LANES:(s + 1) * LANES]
            kz = (ks.astype(F32) * zeta_ref[s]).astype(BF16)
            for hh in range(2):
                h = 2 * s + hh
                mask = lo if hh == 0 else jnp.logical_not(lo)
                qm = jnp.where(mask, qs, jnp.zeros_like(qs))
                vh = v_ref[rs, h * R_V_DIM:(h + 1) * R_V_DIM]
                sc = lax.dot_general(qm, ks, (((1,), (1,)), ((), ())), preferred_element_type=F32)
                sc = (sc * dec_ref[h]).astype(BF16)
                y = jnp.dot(sc, vh, preferred_element_type=F32)
                qx = (qm.astype(F32) * xi_ref[s]).astype(BF16)
                st = state[h]
                y = y + jnp.dot(qx, st.astype(BF16), preferred_element_type=F32)
                kv = lax.dot_general(kz, vh, (((0,), (0,)), ((), ())), preferred_element_type=F32)
                state[h] = cdec_ref[h] * st + kv
                mu = jnp.mean(y, axis=-1, keepdims=True)
                yc = y - mu
                var = jnp.mean(yc * yc, axis=-1, keepdims=True)
                yn = yc * lax.rsqrt(var + 1e-5) * gn_ref[:, h * R_V_DIM:(h + 1) * R_V_DIM]
                g = g_ref[rs, h * R_V_DIM:(h + 1) * R_V_DIM].astype(F32)
                o_ref[rs, h * R_V_DIM:(h + 1) * R_V_DIM] = (g * _sigmoid(g) * yn).astype(o_ref.dtype)


def _retention(qr, kr, vr, gr, gn_g, batch, seq, tr, chunk):
    t = qr.shape[0]
    nt = seq // tr
    decay, zeta, xi, cdec = _retention_tables(chunk)
    row = lambda b, i: (b * nt + i, 0)
    fix2 = lambda b, i: (0, 0)
    fix3 = lambda b, i: (0, 0, 0)
    rqk = R_HEADS * R_QK_DIM
    return pl.pallas_call(
        functools.partial(_retention_kernel, chunk=chunk),
        grid=(batch, nt),
        in_specs=[pl.BlockSpec((tr, rqk), row), pl.BlockSpec((tr, rqk), row),
                  pl.BlockSpec((tr, R_WIDTH), row), pl.BlockSpec((tr, R_WIDTH), row),
                  pl.BlockSpec((1, R_WIDTH), fix2),
                  pl.BlockSpec((R_HEADS, chunk, chunk), fix3),
                  pl.BlockSpec((R_HEADS // 2, chunk, LANES), fix3),
                  pl.BlockSpec((R_HEADS // 2, chunk, LANES), fix3),
                  pl.BlockSpec((R_HEADS, 1, LANES), fix3)],
        out_specs=pl.BlockSpec((tr, R_WIDTH), row),
        out_shape=jax.ShapeDtypeStruct((t, R_WIDTH), BF16),
        scratch_shapes=[pltpu.VMEM((R_HEADS, LANES, R_V_DIM), F32)],
        compiler_params=_cparams("arbitrary", "arbitrary"),
        name="retention",
    )(qr, kr, vr, gr, gn_g.reshape(1, R_WIDTH), decay, zeta, xi, cdec)


def _ssm_matrices(a_re, a_im, b_re, b_im, c_re, c_im, log_dt):
    ell = S_CHUNK
    g_n, p_n, c_n = S_GROUPS, S_STATE, S_GROUP
    dt = jnp.exp(log_dt.astype(F32))[:, None]
    lam_re, lam_im = a_re.astype(F32), a_im.astype(F32)
    mag = jnp.exp(lam_re * dt)
    abar_re = mag * jnp.cos(lam_im * dt)
    abar_im = mag * jnp.sin(lam_im * dt)
    den = lam_re * lam_re + lam_im * lam_im
    nr, ni = abar_re - 1.0, abar_im
    f_re = ((nr * lam_re + ni * lam_im) / den)[..., None]
    f_im = ((ni * lam_re - nr * lam_im) / den)[..., None]
    br, bi = b_re.astype(F32), b_im.astype(F32)
    bb_re = f_re * br - f_im * bi
    bb_im = f_re * bi + f_im * br
    j = jnp.arange(ell + 1, dtype=F32)[:, None, None]
    pw_mag = jnp.exp(j * (lam_re * dt)[None])
    pw_ang = j * (lam_im * dt)[None]
    pw_re = pw_mag * jnp.cos(pw_ang)
    pw_im = pw_mag * jnp.sin(pw_ang)
    cr, ci = c_re.astype(F32), c_im.astype(F32)
    hi = lax.Precision.HIGHEST
    w_re = cr[None] * pw_re[:, :, None, :] - ci[None] * pw_im[:, :, None, :]
    w_im = cr[None] * pw_im[:, :, None, :] + ci[None] * pw_re[:, :, None, :]
    kern = (jnp.einsum('jgcp,gpd->jgcd', w_re[:ell], bb_re, precision=hi)
            - jnp.einsum('jgcp,gpd->jgcd', w_im[:ell], bb_im, precision=hi))
    s_i = np.arange(ell)[:, None]
    t_i = np.arange(ell)[None, :]
    lag = t_i - s_i
    kt = kern[np.clip(lag, 0, None)]
    kt = jnp.where(jnp.asarray(lag >= 0)[:, :, None, None, None], kt, 0.0)
    toep = kt.transpose(2, 0, 4, 1, 3).reshape(g_n, ell * c_n, ell * c_n)
    e_re = pw_re[ell - 1 - np.arange(ell)]
    e_im = pw_im[ell - 1 - np.arange(ell)]
    bz_re = (e_re[:, :, :, None] * bb_re[None] - e_im[:, :, :, None] * bb_im[None])
    bz_im = (e_re[:, :, :, None] * bb_im[None] + e_im[:, :, :, None] * bb_re[None])
    bz_re = bz_re.transpose(1, 0, 3, 2).reshape(g_n, ell * c_n, p_n)
    bz_im = bz_im.transpose(1, 0, 3, 2).reshape(g_n, ell * c_n, p_n)
    zeros = jnp.zeros_like(bz_re[0::2])
    top = jnp.concatenate([bz_re[0::2], zeros, bz_im[0::2], zeros], axis=-1)
    bot = jnp.concatenate([zeros, bz_re[1::2], zeros, bz_im[1::2]], axis=-1)
    bz = jnp.concatenate([top, bot], axis=1)
    cz_re = w_re[1:].transpose(1, 3, 0, 2).reshape(g_n, p_n, ell * c_n)
    cz_im = -w_im[1:].transpose(1, 3, 0, 2).reshape(g_n, p_n, ell * c_n)
    zc = jnp.zeros_like(cz_re[0::2])
    cz = jnp.concatenate([
        jnp.concatenate([cz_re[0::2], zc], axis=-1),
        jnp.concatenate([zc, cz_re[1::2]], axis=-1),
        jnp.concatenate([cz_im[0::2], zc], axis=-1),
        jnp.concatenate([zc, cz_im[1::2]], axis=-1)], axis=1)
    al_re = pw_re[ell].reshape(S_PAIRS, 2 * p_n)
    al_im = pw_im[ell].reshape(S_PAIRS, 2 * p_n)
    a_l = jnp.stack([al_re, al_im], axis=0)
    return toep.astype(BF16), bz.astype(BF16), cz.astype(BF16), a_l.astype(F32)


def _ssm_state_kernel(u_ref, bz_ref, s_ref):
    s_ref[...] = jnp.dot(u_ref[...], bz_ref[0], preferred_element_type=F32)


def _ssm_scan_kernel(s_ref, al_ref, h_ref, *, batch, n_chunks):
    n_blk = 2 * S_PAIRS
    a_re = [al_ref[0, k:k + 1, :] for k in range(S_PAIRS)]
    a_im = [al_ref[1, k:k + 1, :] for k in range(S_PAIRS)]

    def body(n, carry):
        new = []
        for b in range(batch):
            row = b * n_chunks + n
            h = carry[b * n_blk:(b + 1) * n_blk]
            s_row = s_ref[pl.ds(row, 1), :]
            h_ref[pl.ds(row, 1), :] = jnp.concatenate(h, axis=1)
            for k in range(S_PAIRS):
                hr, hi = h[2 * k], h[2 * k + 1]
                sr = s_row[:, (2 * k) * LANES:(2 * k + 1) * LANES]
                si = s_row[:, (2 * k + 1) * LANES:(2 * k + 2) * LANES]
                new.append(a_re[k] * hr - a_im[k] * hi + sr)
                new.append(a_re[k] * hi + a_im[k] * hr + si)
        return tuple(new)

    init = tuple(jnp.zeros((1, LANES), F32) for _ in range(batch * n_blk))
    lax.fori_loop(0, n_chunks, body, init)


def _ssm_out_kernel(u_ref, toep_ref, h_ref, cz_ref, y_ref):
    half = S_CHUNK * S_GROUP
    cross = jnp.dot(h_ref[...].astype(BF16), cz_ref[0], preferred_element_type=F32)
    for g in range(2):
        sl = slice(g * half, (g + 1) * half)
        y = jnp.dot(u_ref[:, sl], toep_ref[g], preferred_element_type=F32) + cross[:, sl]
        y_ref[:, sl] = y.astype(y_ref.dtype)


def _ssm_conv(us, mats, batch, seq):
    toep, bz, cz, a_l = mats
    t = us.shape[0]
    ell = S_CHUNK
    n_chunks = seq // ell
    n_all = batch * n_chunks
    pair_w = 2 * ell * S_GROUP
    u_t = us.astype(BF16).reshape(n_all, ell, S_PAIRS, 2, S_GROUP).transpose(0, 2, 3, 1, 4)
    u_t = u_t.reshape(n_all, S_PAIRS * pair_w)
    st_w = 4 * S_STATE
    s_all = pl.pallas_call(
        _ssm_state_kernel,
        grid=(S_PAIRS,),
        in_specs=[pl.BlockSpec((n_all, pair_w), lambda k: (0, k)),
                  pl.BlockSpec((1, pair_w, st_w), lambda k: (k, 0, 0))],
        out_specs=pl.BlockSpec((n_all, st_w), lambda k: (0, k)),
        out_shape=jax.ShapeDtypeStruct((n_all, S_PAIRS * st_w), F32),
        compiler_params=_cparams("arbitrary"),
        name="ssm_chunk_state",
    )(u_t, bz)
    h_prev = pl.pallas_call(
        functools.partial(_ssm_scan_kernel, batch=batch, n_chunks=n_chunks),
        out_shape=jax.ShapeDtypeStruct((n_all, S_PAIRS * st_w), F32),
        compiler_params=pltpu.CompilerParams(vmem_limit_bytes=VMEM_LIMIT),
        name="ssm_chunk_scan",
    )(s_all, a_l)
    y_t = pl.pallas_call(
        _ssm_out_kernel,
        grid=(S_PAIRS,),
        in_specs=[pl.BlockSpec((n_all, pair_w), lambda k: (0, k)),
                  pl.BlockSpec((2, pair_w // 2, pair_w // 2), lambda k: (k, 0, 0)),
                  pl.BlockSpec((n_all, st_w), lambda k: (0, k)),
                  pl.BlockSpec((1, st_w, pair_w), lambda k: (k, 0, 0))],
        out_specs=pl.BlockSpec((n_all, pair_w), lambda k: (0, k)),
        out_shape=jax.ShapeDtypeStruct((n_all, S_PAIRS * pair_w), BF16),
        compiler_params=_cparams("arbitrary"),
        name="ssm_chunk_out",
    )(u_t, toep, h_prev, cz)
    y = y_t.reshape(n_all, S_PAIRS, 2, ell, S_GROUP).transpose(0, 3, 1, 2, 4)
    return y.reshape(t, S_CHANNELS)


def _out_proj_kernel(*refs, with_router):
    if with_router:
        (h_ref, ya_ref, yr_ref, ys_ref, us_ref, d_ref, gw_ref, gb_ref, w_ref, n2_ref, rt_ref,
         h1_ref, xn_ref, lg_ref) = refs
    else:
        (h_ref, ya_ref, yr_ref, ys_ref, us_ref, d_ref, gw_ref, gb_ref, w_ref, n2_ref,
         h1_ref, xn_ref) = refs
    y = ys_ref[...].astype(F32) + d_ref[...] * us_ref[...]
    z = 0.5 * y * (1.0 + jnp.tanh(math.sqrt(2.0 / math.pi) * (y + 0.044715 * (y * y * y))))
    gate = jnp.dot(z.astype(BF16), gw_ref[...], preferred_element_type=F32) + gb_ref[...]
    yc = (z * _sigmoid(gate)).astype(BF16)
    c1 = A_WIDTH
    c2 = A_WIDTH + R_WIDTH
    acc = jnp.dot(ya_ref[...], w_ref[0:c1, :], preferred_element_type=F32)
    acc = acc + jnp.dot(yr_ref[...], w_ref[c1:c2, :], preferred_element_type=F32)
    acc = acc + jnp.dot(yc, w_ref[c2:, :], preferred_element_type=F32)
    h1 = h_ref[...] + acc
    h1_ref[...] = h1
    ms = jnp.mean(h1 * h1, axis=-1, keepdims=True)
    xn = h1 * lax.rsqrt(ms + EPS) * n2_ref[...]
    if not with_router:
        xn_ref[...] = xn.astype(xn_ref.dtype)
    else:
        n_slab = xn.shape[1] // LANES
        for c in range(n_slab):
            xn_ref[pl.ds(c, xn.shape[0], stride=n_slab), :] = xn[:, c * LANES:(c + 1) * LANES]
        x_hi = xn.astype(BF16)
        x_lo = (xn - x_hi.astype(F32)).astype(BF16)
        both = jnp.dot(x_hi, rt_ref[...], preferred_element_type=F32)
        lg_ref[...] = (both[:, :LANES] + both[:, LANES:]
                       + jnp.dot(x_lo, rt_ref[:, :LANES], preferred_element_type=F32))


def _out_proj(h2d, ya, yr, ys, us, d_skip, glu_w, glu_b, w_out, norm2_g, router, tm):
    t, d = h2d.shape
    with_router = router is not None
    row = lambda i: (i, 0)
    fixed = lambda i: (0, 0)
    in_specs = [pl.BlockSpec((tm, d), row), pl.BlockSpec((tm, A_WIDTH), row), pl.BlockSpec((tm, R_WIDTH), row),
                pl.BlockSpec((tm, S_CHANNELS), row), pl.BlockSpec((tm, S_CHANNELS), row),
                pl.BlockSpec((1, S_CHANNELS), fixed), pl.BlockSpec((S_CHANNELS, S_CHANNELS), fixed),
                pl.BlockSpec((1, S_CHANNELS), fixed), pl.BlockSpec(w_out.shape, fixed), pl.BlockSpec((1, d), fixed)]
    args = [h2d, ya, yr, ys, us, d_skip.reshape(1, -1), glu_w, glu_b.reshape(1, -1), w_out, norm2_g.reshape(1, d)]
    out_shapes = [jax.ShapeDtypeStruct((t, d), F32), jax.ShapeDtypeStruct((t, d), BF16)]
    out_specs = [pl.BlockSpec((tm, d), row), pl.BlockSpec((tm, d), row)]
    if with_router:
        n_slab = d // LANES
        out_shapes[1] = jax.ShapeDtypeStruct((t * n_slab, LANES), F32)
        out_specs[1] = pl.BlockSpec((tm * n_slab, LANES), row)
        rt = jnp.pad(router.astype(F32), ((0, 0), (0, LANES - router.shape[1])))
        rt_hi = rt.astype(BF16)
        rt_lo = (rt - rt_hi.astype(F32)).astype(BF16)
        in_specs.append(pl.BlockSpec((d, 2 * LANES), fixed))
        args.append(jnp.concatenate([rt_hi, rt_lo], axis=1))
        out_shapes.append(jax.ShapeDtypeStruct((t, LANES), F32))
        out_specs.append(pl.BlockSpec((tm, LANES), row))
    return pl.pallas_call(
        functools.partial(_out_proj_kernel, with_router=with_router),
        grid=(t // tm,),
        in_specs=in_specs,
        out_specs=out_specs,
        out_shape=out_shapes,
        compiler_params=_cparams("arbitrary"),
        name="out_proj",
    )(*args)


def _swiglu_chunk(x, wg, wu, wd):
    hg = jnp.dot(x, wg, preferred_element_type=F32)
    hu = jnp.dot(x, wu, preferred_element_type=F32)
    a = (hg * _sigmoid(hg) * hu).astype(BF16)
    return jnp.dot(a, wd, preferred_element_type=F32)


def _ffn_kernel(x_ref, h_ref, wg_ref, wu_ref, wd_ref, o_ref):
    @pl.when(pl.program_id(1) == 0)
    def _():
        o_ref[...] = h_ref[...]

    o_ref[...] += _swiglu_chunk(x_ref[...], wg_ref[...], wu_ref[...], wd_ref[...])


def _ffn(xn, h1, wg, wu, wd, tm, fc):
    t, d = xn.shape
    f = wg.shape[1]
    return pl.pallas_call(
        _ffn_kernel,
        grid=(t // tm, f // fc),
        in_specs=[pl.BlockSpec((tm, d), lambda i, j: (i, 0)), pl.BlockSpec((tm, d), lambda i, j: (i, 0)),
                  pl.BlockSpec((d, fc), lambda i, j: (0, j)), pl.BlockSpec((d, fc), lambda i, j: (0, j)),
                  pl.BlockSpec((fc, d), lambda i, j: (j, 0))],
        out_specs=pl.BlockSpec((tm, d), lambda i, j: (i, 0)),
        out_shape=jax.ShapeDtypeStruct((t, d), F32),
        compiler_params=_cparams("arbitrary", "arbitrary"),
        name="swiglu_ffn",
    )(xn, h1, wg, wu, wd)


def _route_kernel(lg_ref, tri_ref, gate_ref, idx_ref, cnt_ref, carry):
    @pl.when(pl.program_id(0) == 0)
    def _():
        carry[...] = jnp.zeros(carry.shape, F32)

    lg = lg_ref[...]
    lane = lax.broadcasted_iota(jnp.int32, lg.shape, 1)
    valid = lane < N_EXPERTS
    mx = jnp.max(jnp.where(valid, lg, NEG_BIG), axis=-1, keepdims=True)
    ex = jnp.where(valid, jnp.exp(lg - mx), 0.0)
    probs = ex / jnp.sum(ex, axis=-1, keepdims=True)
    p1 = jnp.max(probs, axis=-1, keepdims=True)
    e1 = jnp.min(jnp.where(valid & (probs == p1), lane, LANES), axis=-1, keepdims=True)
    rest = jnp.where(valid & (lane != e1), probs, -1.0)
    p2 = jnp.max(rest, axis=-1, keepdims=True)
    e2 = jnp.min(jnp.where(rest == p2, lane, LANES), axis=-1, keepdims=True)
    den = p1 + p2
    oh1 = lane == e1
    oh2 = lane == e2
    oh = jnp.where(oh1 | oh2, 1.0, 0.0)
    cum = jnp.dot(tri_ref[...], oh.astype(BF16), preferred_element_type=F32)
    excl = cum - oh + carry[...]
    r1 = jnp.sum(jnp.where(oh1, excl, 0.0), axis=-1, keepdims=True)
    r2 = jnp.sum(jnp.where(oh2, excl, 0.0), axis=-1, keepdims=True)
    tot = carry[...] + cum[cum.shape[0] - 1:cum.shape[0], :]
    carry[...] = tot
    cnt_ref[...] = tot.astype(jnp.int32)
    gate_ref[...] = jnp.where(lane == 0, p1 / den, jnp.where(lane == 1, p2 / den, 0.0))
    idx_ref[...] = jnp.where(lane == 0, e1, jnp.where(lane == 1, e2, jnp.where(
        lane == 2, r1.astype(jnp.int32), jnp.where(lane == 3, r2.astype(jnp.int32), 0))))


def _route(logits, tm):
    t = logits.shape[0]
    tri = jnp.asarray(np.tril(np.ones((tm, tm), np.float32)), BF16)
    row = lambda i: (i, 0)
    return pl.pallas_call(
        _route_kernel,
        grid=(t // tm,),
        in_specs=[pl.BlockSpec((tm, LANES), row), pl.BlockSpec((tm, tm), lambda i: (0, 0))],
        out_specs=[pl.BlockSpec((tm, LANES), row), pl.BlockSpec((tm, LANES), row),
                   pl.BlockSpec((1, LANES), lambda i: (0, 0))],
        out_shape=[jax.ShapeDtypeStruct((t, LANES), F32), jax.ShapeDtypeStruct((t, LANES), jnp.int32),
                   jax.ShapeDtypeStruct((1, LANES), jnp.int32)],
        scratch_shapes=[pltpu.VMEM((1, LANES), F32)],
        compiler_params=_cparams("arbitrary"),
        name="moe_route",
    )(logits, tri)


DMA_UNROLL = 8
SLABS = 8


def _slab_rows(row):
    return pl.ds(pl.multiple_of(row * SLABS, SLABS), SLABS)


def _dispatch_kernel(pos_ref, fill_ref, nu_ref, x_ref, zero_hbm, xs_hbm, sem, *, td, tile_rows, n_tiles):
    i = pl.program_id(0)

    def fill_copy(tile):
        start = pl.multiple_of(tile * (tile_rows * SLABS), tile_rows * SLABS)
        return pltpu.make_async_copy(zero_hbm, xs_hbm.at[pl.ds(start, tile_rows * SLABS)], sem)

    @pl.when(i == 0)
    def _():
        for e in range(N_EXPERTS):
            @pl.when(fill_ref[e] >= 0)
            def _():
                fill_copy(fill_ref[e]).start()
        for e in range(N_EXPERTS):
            @pl.when(fill_ref[e] >= 0)
            def _():
                fill_copy(fill_ref[e]).wait()

        def clear_unused(tile, c):
            fill_copy(tile).start()
            fill_copy(tile).wait()
            return c

        lax.fori_loop(nu_ref[0], n_tiles, clear_unused, 0)

    def issue(jb, c):
        for u in range(DMA_UNROLL):
            j = jb * DMA_UNROLL + u
            for slot in range(2):
                dst = pos_ref[2 * (i * td + j) + slot]
                pltpu.make_async_copy(x_ref.at[_slab_rows(j)], xs_hbm.at[_slab_rows(dst)], sem).start(priority=slot)
        return c

    lax.fori_loop(0, td // DMA_UNROLL, issue, 0)
    for slot in range(2):
        pltpu.make_async_copy(x_ref, xs_hbm.at[pl.ds(0, td * SLABS)], sem).wait()


def _dispatch(xn, pos_flat, fill_tile, n_used, n_tiles, tile_rows, td):
    zeros = jnp.zeros((tile_rows * SLABS, LANES), xn.dtype)
    return pl.pallas_call(
        functools.partial(_dispatch_kernel, td=td, tile_rows=tile_rows, n_tiles=n_tiles),
        grid_spec=pltpu.PrefetchScalarGridSpec(
            num_scalar_prefetch=3,
            grid=(xn.shape[0] // (td * SLABS),),
            in_specs=[pl.BlockSpec((td * SLABS, LANES), lambda i, p, f, nu: (i, 0)),
                      pl.BlockSpec(memory_space=pl.ANY)],
            out_specs=pl.BlockSpec(memory_space=pl.ANY),
            scratch_shapes=[pltpu.SemaphoreType.DMA(())]),
        out_shape=jax.ShapeDtypeStruct((n_tiles * tile_rows * SLABS, LANES), xn.dtype),
        compiler_params=pltpu.CompilerParams(dimension_semantics=("arbitrary",)),
        name="moe_dispatch",
    )(pos_flat, fill_tile, n_used, xn, zeros)


def _moe_kernel(te_ref, nu_ref, x_ref, wg_ref, wu_ref, wd_ref, o_ref, xb, acc):
    i = pl.program_id(0)
    f = pl.program_id(1)
    tm = xb.shape[0]

    @pl.when(i < nu_ref[0])
    def _():
        @pl.when(f == 0)
        def _():
            for c in range(SLABS):
                xb[:, c * LANES:(c + 1) * LANES] = x_ref[pl.ds(c, tm, stride=SLABS), :].astype(BF16)
            acc[...] = jnp.zeros(acc.shape, F32)

        acc[...] += _swiglu_chunk(xb[...], wg_ref[0], wu_ref[0], wd_ref[0])

        @pl.when(f == pl.num_programs(1) - 1)
        def _():
            for c in range(SLABS):
                o_ref[pl.ds(c, tm, stride=SLABS), :] = acc[:, c * LANES:(c + 1) * LANES]

    @pl.when(jnp.logical_and(i >= nu_ref[0], f == 0))
    def _():
        o_ref[...] = jnp.zeros(o_ref.shape, F32)


def _moe_experts(xs, tile_expert, n_used, wg, wu, wd, tm, fc):
    d = wg.shape[1]
    f = wg.shape[2]
    nf = f // fc
    n_tiles = xs.shape[0] // (tm * SLABS)

    def x_map(i, j, te, nu):
        return (jnp.minimum(i, nu[0] - 1), 0)

    def f_eff(i, j, nu):
        return jnp.where(i < nu[0], j, nf - 1)

    return pl.pallas_call(
        _moe_kernel,
        grid_spec=pltpu.PrefetchScalarGridSpec(
            num_scalar_prefetch=2,
            grid=(n_tiles, nf),
            in_specs=[pl.BlockSpec((tm * SLABS, LANES), x_map),
                      pl.BlockSpec((1, d, fc), lambda i, j, te, nu: (te[i], 0, f_eff(i, j, nu))),
                      pl.BlockSpec((1, d, fc), lambda i, j, te, nu: (te[i], 0, f_eff(i, j, nu))),
                      pl.BlockSpec((1, fc, d), lambda i, j, te, nu: (te[i], f_eff(i, j, nu), 0))],
            out_specs=pl.BlockSpec((tm * SLABS, LANES), lambda i, j, te, nu: (i, 0)),
            scratch_shapes=[pltpu.VMEM((tm, d), BF16), pltpu.VMEM((tm, d), F32)]),
        out_shape=jax.ShapeDtypeStruct(xs.shape, F32),
        compiler_params=_cparams("arbitrary", "arbitrary"),
        name="moe_experts",
    )(tile_expert, n_used, xs, wg, wu, wd)


def _combine_kernel(pos_ref, h_ref, gate_ref, ys_hbm, o_ref, buf, sem, *, tc):
    i = pl.program_id(0)
    n = pl.num_programs(0)

    def issue(step, slot_buf):
        def body(jb, c):
            for u in range(DMA_UNROLL):
                j = jb * DMA_UNROLL + u
                for k in range(2):
                    src = pos_ref[2 * (step * tc + j) + k]
                    pltpu.make_async_copy(ys_hbm.at[_slab_rows(src)], buf.at[slot_buf, k, _slab_rows(j)],
                                          sem.at[slot_buf]).start(priority=k)
            return c
        lax.fori_loop(0, tc // DMA_UNROLL, body, 0)

    @pl.when(i == 0)
    def _():
        issue(0, 0)

    @pl.when(i + 1 < n)
    def _():
        issue(i + 1, (i + 1) % 2)

    cur = i % 2
    for k in range(2):
        pltpu.make_async_copy(ys_hbm.at[pl.ds(0, tc * SLABS)], buf.at[cur, k], sem.at[cur]).wait()
    g = gate_ref[...]
    for c in range(SLABS):
        cols = slice(c * LANES, (c + 1) * LANES)
        o_ref[:, cols] = (h_ref[:, cols] + g[:, 0:1] * buf[cur, 0, pl.ds(c, tc, stride=SLABS), :]
                          + g[:, 1:2] * buf[cur, 1, pl.ds(c, tc, stride=SLABS), :])


def _combine(h1, gates, ys, pos_flat, tc):
    t, d = h1.shape
    row = lambda i, p: (i, 0)
    return pl.pallas_call(
        functools.partial(_combine_kernel, tc=tc),
        grid_spec=pltpu.PrefetchScalarGridSpec(
            num_scalar_prefetch=1,
            grid=(t // tc,),
            in_specs=[pl.BlockSpec((tc, d), row), pl.BlockSpec((tc, LANES), row),
                      pl.BlockSpec(memory_space=pl.ANY)],
            out_specs=pl.BlockSpec((tc, d), row),
            scratch_shapes=[pltpu.VMEM((2, 2, tc * SLABS, LANES), F32), pltpu.SemaphoreType.DMA((2,))]),
        out_shape=jax.ShapeDtypeStruct((t, d), F32),
        compiler_params=_cparams("arbitrary"),
        name="moe_combine",
    )(pos_flat, h1, gates, ys)


def _moe(xn, h1, logits, wg, wu, wd, tm, fc):
    t = h1.shape[0]
    gates, idx, counts = _route(logits, 512)
    counts = counts[0, :N_EXPERTS]
    tiles_per = (counts + tm - 1) // tm
    tile_end = jnp.cumsum(tiles_per)
    group_start = (tile_end - tiles_per) * tm
    pos = group_start[idx[:, 0:2]] + idx[:, 2:4]
    pos_flat = pos.reshape(-1).astype(jnp.int32)
    n_tiles = (2 * t) // tm + N_EXPERTS
    n_used = tile_end[-1].astype(jnp.int32).reshape(1)
    tile_ids = jnp.minimum(jnp.arange(n_tiles, dtype=jnp.int32), n_used[0] - 1)
    tile_expert = jnp.sum(tile_ids[:, None] >= tile_end[None, :], axis=1).astype(jnp.int32)
    fill_tile = jnp.where(counts > 0, tile_end - 1, -1).astype(jnp.int32)
    xs = _dispatch(xn, pos_flat, fill_tile, n_used, n_tiles, tm, 512)
    ys = _moe_experts(xs, tile_expert, n_used, wg, wu, wd, tm, fc)
    return _combine(h1, gates, ys, pos_flat, 256)


def _trunk(x, norm1_g, w_in, q_norm_g, k_norm_g, ret_gn_g, ssm_a_re, ssm_a_im, ssm_b_re, ssm_b_im,
           ssm_c_re, ssm_c_im, ssm_d, ssm_log_dt, ssm_glu_w, ssm_glu_b, w_out, norm2_g,
           ffn_w_gate, ffn_w_up, ffn_w_down, moe_router, moe_w_gate, moe_w_up, moe_w_down,
           *, tm=512, tr=512, r_chunk=128, ffn_fc=1408, moe_tm=512, moe_fc=1792):
    batch, seq, d = x.shape
    depth = norm1_g.shape[0]
    t = batch * seq
    tabs = _rope_tables(seq, ROPE_DIM, ROPE_THETA) + _rope_tables(seq, R_QK_DIM, R_ROPE_THETA)
    h = x.reshape(t, d).astype(F32)
    for layer in range(depth):
        qa, ka, va, qr, kr, vr, gr, us = _in_proj(
            h, seq, norm1_g[layer], w_in[layer].astype(BF16), q_norm_g[layer], k_norm_g[layer], tabs, tm)
        ya = _attention(qa, ka, va, batch, seq)
        yr = _retention(qr, kr, vr, gr, ret_gn_g[layer], batch, seq, tr, r_chunk)
        mats = _ssm_matrices(ssm_a_re[layer], ssm_a_im[layer], ssm_b_re[layer], ssm_b_im[layer],
                             ssm_c_re[layer], ssm_c_im[layer], ssm_log_dt[layer])
        ys = _ssm_conv(us, mats, batch, seq)
        i = layer // 2
        router = moe_router[i] if layer % 2 == 1 else None
        outs = _out_proj(h, ya, yr, ys, us, ssm_d[layer], ssm_glu_w[layer].astype(BF16), ssm_glu_b[layer],
                         w_out[layer].astype(BF16), norm2_g[layer], router, tm)
        if layer % 2 == 0:
            h1, xn = outs
            h = _ffn(xn, h1, ffn_w_gate[i].astype(BF16), ffn_w_up[i].astype(BF16), ffn_w_down[i].astype(BF16),
                     2 * tm, ffn_fc)
        else:
            h1, xn, logits = outs
            h = _moe(xn, h1, logits, moe_w_gate[i].astype(BF16), moe_w_up[i].astype(BF16),
                     moe_w_down[i].astype(BF16), moe_tm, moe_fc)
    return h.reshape(batch, seq, d).astype(x.dtype)


def kernel(x, norm1_g, w_in, q_norm_g, k_norm_g, ret_gn_g, ssm_a_re, ssm_a_im, ssm_b_re, ssm_b_im, ssm_c_re,
           ssm_c_im, ssm_d, ssm_log_dt, ssm_glu_w, ssm_glu_b, w_out, norm2_g, ffn_w_gate, ffn_w_up, ffn_w_down,
           moe_router, moe_w_gate, moe_w_up, moe_w_down):
    return _trunk(x, norm1_g, w_in, q_norm_g, k_norm_g, ret_gn_g, ssm_a_re, ssm_a_im, ssm_b_re, ssm_b_im,
                  ssm_c_re, ssm_c_im, ssm_d, ssm_log_dt, ssm_glu_w, ssm_glu_b, w_out, norm2_g,
                  ffn_w_gate, ffn_w_up, ffn_w_down, moe_router, moe_w_gate, moe_w_up, moe_w_down)
```

```python
import functools
import math

import jax
import jax.numpy as jnp
import numpy as np
from jax import lax
from jax.experimental import pallas as pl
from jax.experimental.pallas import tpu as pltpu

F32 = jnp.float32
BF16 = jnp.bfloat16

LANES = 128
EPS = 1e-6
HEAD_DIM = 64
A_HEADS = 4
A_BLOCK = 128
A_DILATIONS = (1, 4, 16)
A_SPAN = A_BLOCK * max(A_DILATIONS)
ROPE_THETA = 500000.0
ROPE_DIM = HEAD_DIM // 4
R_HEADS = 4
R_QK_DIM = 64
R_V_DIM = 128
R_ROPE_THETA = 10000.0
S_CHANNELS = 256
S_GROUP = 16
S_GROUPS = S_CHANNELS // S_GROUP
S_STATE = 64
S_CHUNK = 32
S_PAIRS = S_GROUPS // 2
A_WIDTH = A_HEADS * HEAD_DIM
R_WIDTH = R_HEADS * R_V_DIM
N_EXPERTS = 8
NEG_BIG = -1e30

VMEM_LIMIT = 56 * 1024 * 1024


def _cparams(*sem):
    return pltpu.CompilerParams(dimension_semantics=sem, vmem_limit_bytes=VMEM_LIMIT)


def _sigmoid(x):
    return 1.0 / (1.0 + jnp.exp(-x))


def _rope_tables(seq, rot_dim, theta):
    half = rot_dim // 2
    inv = jnp.power(theta, -jnp.arange(half, dtype=F32) * 2.0 / rot_dim)
    ang = jnp.arange(seq, dtype=jnp.int32).astype(F32)[:, None] * inv[None, :]
    d = np.arange(LANES) % HEAD_DIM
    idx = jnp.asarray(d % half)
    cos = jnp.cos(ang)[:, idx]
    sin = jnp.sin(ang)[:, idx]
    in_rot = jnp.asarray(d < rot_dim)[None, :]
    first = jnp.asarray(d < half)[None, :]
    cos_t = jnp.where(in_rot, cos, 1.0)
    sin_t = jnp.where(in_rot, jnp.where(first, -sin, sin), 0.0)
    return cos_t.astype(F32), sin_t.astype(F32)


def _rope_slab(x, cos_t, sin_t, half):
    lane = lax.broadcasted_iota(jnp.int32, x.shape, 1)
    fwd = pltpu.roll(x, LANES - half, 1)
    bwd = pltpu.roll(x, half, 1)
    partner = jnp.where((lane % (2 * half)) < half, fwd, bwd)
    return x * cos_t + partner * sin_t


def _head_rms_slab(x, g):
    lane = lax.broadcasted_iota(jnp.int32, x.shape, 1)
    lo = lane < HEAD_DIM
    x2 = x * x
    s0 = jnp.sum(jnp.where(lo, x2, 0.0), axis=-1, keepdims=True)
    s1 = jnp.sum(jnp.where(lo, 0.0, x2), axis=-1, keepdims=True)
    ms = jnp.where(lo, s0, s1) * (1.0 / HEAD_DIM)
    return x * lax.rsqrt(ms + EPS) * g


def _in_proj_kernel(x_ref, g_ref, w_ref, qg_ref, kg_ref, ca_ref, sa_ref, cr_ref, sr_ref,
                    qa_ref, ka_ref, va_ref, qr_ref, kr_ref, vr_ref, gr_ref, us_ref):
    x = x_ref[...]
    ms = jnp.mean(x * x, axis=-1, keepdims=True)
    xn = (x * lax.rsqrt(ms + EPS) * g_ref[...]).astype(BF16)

    def proj(c0, n):
        return jnp.dot(xn, w_ref[:, c0:c0 + n], preferred_element_type=F32)

    ca, sa, cr, sr = ca_ref[...], sa_ref[...], cr_ref[...], sr_ref[...]
    qa = proj(0, A_WIDTH)
    ka = proj(A_WIDTH, A_WIDTH)
    for s in range(A_WIDTH // LANES):
        sl = slice(s * LANES, (s + 1) * LANES)
        qn = _rope_slab(_head_rms_slab(qa[:, sl], qg_ref[...]), ca, sa, ROPE_DIM // 2)
        qa_ref[:, sl] = qn * (math.log2(math.e) * HEAD_DIM ** -0.5)
        ka_ref[:, sl] = _rope_slab(_head_rms_slab(ka[:, sl], kg_ref[...]), ca, sa, ROPE_DIM // 2)
    va_ref[...] = proj(2 * A_WIDTH, A_WIDTH)
    c0 = 3 * A_WIDTH
    rqk = R_HEADS * R_QK_DIM
    qr = proj(c0, rqk)
    kr = proj(c0 + rqk, rqk)
    for s in range(rqk // LANES):
        sl = slice(s * LANES, (s + 1) * LANES)
        qr_ref[:, sl] = _rope_slab(qr[:, sl], cr, sr, R_QK_DIM // 2).astype(BF16)
        kr_ref[:, sl] = (_rope_slab(kr[:, sl], cr, sr, R_QK_DIM // 2) * (R_QK_DIM ** -0.5)).astype(BF16)
    c0 += 2 * rqk
    vr_ref[...] = proj(c0, R_WIDTH).astype(BF16)
    gr_ref[...] = proj(c0 + R_WIDTH, R_WIDTH).astype(BF16)
    us_ref[...] = proj(c0 + 2 * R_WIDTH, S_CHANNELS)


def _in_proj(h2d, seq, norm_g, w_in, q_g, k_g, tabs, tm):
    t, d = h2d.shape
    n_cols = w_in.shape[1]
    nt_seq = seq // tm
    row = lambda i: (i, 0)
    fixed = lambda i: (0, 0)
    tab = lambda i: (i % nt_seq, 0)
    out_shapes = (
        jax.ShapeDtypeStruct((t, A_WIDTH), F32), jax.ShapeDtypeStruct((t, A_WIDTH), F32),
        jax.ShapeDtypeStruct((t, A_WIDTH), F32),
        jax.ShapeDtypeStruct((t, R_HEADS * R_QK_DIM), BF16), jax.ShapeDtypeStruct((t, R_HEADS * R_QK_DIM), BF16),
        jax.ShapeDtypeStruct((t, R_WIDTH), BF16), jax.ShapeDtypeStruct((t, R_WIDTH), BF16),
        jax.ShapeDtypeStruct((t, S_CHANNELS), F32))
    return pl.pallas_call(
        _in_proj_kernel,
        grid=(t // tm,),
        in_specs=[pl.BlockSpec((tm, d), row), pl.BlockSpec((1, d), fixed), pl.BlockSpec((d, n_cols), fixed),
                  pl.BlockSpec((1, LANES), fixed), pl.BlockSpec((1, LANES), fixed)]
                 + [pl.BlockSpec((tm, LANES), tab)] * 4,
        out_specs=[pl.BlockSpec((tm, s.shape[1]), row) for s in out_shapes],
        out_shape=out_shapes,
        compiler_params=_cparams("arbitrary"),
        name="in_proj",
    )(h2d, norm_g.reshape(1, d), w_in, jnp.tile(q_g, 2).reshape(1, LANES), jnp.tile(k_g, 2).reshape(1, LANES), *tabs)


def _attn_kernel(q_ref, k_ref, v_ref, o_ref, kbuf, vbuf, acc, mst, lst, bias, *, unroll):
    i = pl.program_id(2)
    span = A_SPAN

    @pl.when(i == 0)
    def _():
        kbuf[0:span, :] = jnp.zeros((span, LANES), F32)
        vbuf[0:span, :] = jnp.zeros((span, LANES), F32)

    @pl.when(i > 0)
    def _():
        kbuf[0:span, :] = kbuf[span:2 * span, :]
        vbuf[0:span, :] = vbuf[span:2 * span, :]

    kbuf[span:2 * span, :] = k_ref[...]
    vbuf[span:2 * span, :] = v_ref[...]

    qi = lax.broadcasted_iota(jnp.int32, (A_BLOCK, 2 * A_BLOCK), 0)
    kj = lax.broadcasted_iota(jnp.int32, (A_BLOCK, 2 * A_BLOCK), 1)
    bias[0] = jnp.where(kj >= qi, jnp.where(kj <= qi + A_BLOCK, 0.0, NEG_BIG), NEG_BIG)
    bias[1] = jnp.where(kj >= jnp.maximum(qi, A_BLOCK), jnp.where(kj <= qi + A_BLOCK, 0.0, NEG_BIG), NEG_BIG)
    lane = lax.broadcasted_iota(jnp.int32, (A_BLOCK, LANES), 1)
    lo = lane < HEAD_DIM

    def rows(start, n, d):
        if d == 1:
            return pl.ds(pl.multiple_of(start, A_BLOCK), n)
        return pl.ds(start, n, stride=d)

    for pi, d in enumerate(A_DILATIONS):
        n_blk = span // A_BLOCK

        def body(blk, carry, d=d, pi=pi):
            if d == 1:
                sp, r = blk, 0
            elif d * A_BLOCK == span:
                sp, r = 0, blk
            else:
                sp, r = blk // d, blk % d
            q0 = sp * (A_BLOCK * d) + r
            qb = q_ref[rows(q0, A_BLOCK, d), :]
            k0 = span + q0 - A_BLOCK * d
            kb = kbuf[rows(k0, 2 * A_BLOCK, d), :].astype(BF16)
            vb = vbuf[rows(k0, 2 * A_BLOCK, d), :].astype(BF16)
            mask = bias[jnp.where(jnp.logical_or(i > 0, sp > 0), 0, 1)]
            ms, ls, os_ = [], [], []
            for hh in range(2):
                qh = jnp.where(lo if hh == 0 else jnp.logical_not(lo), qb, 0.0).astype(BF16)
                s = lax.dot_general(qh, kb, (((1,), (1,)), ((), ())), preferred_element_type=F32) + mask
                m_h = jnp.max(s, axis=-1, keepdims=True)
                p = jnp.exp2(s - m_h)
                ls.append(jnp.sum(p, axis=-1, keepdims=True))
                ms.append(m_h)
                os_.append(jnp.dot(p.astype(BF16), vb, preferred_element_type=F32))
            qrows = rows(q0, A_BLOCK, d)
            acc[pi, qrows, :] = jnp.where(lo, os_[0], os_[1])
            mst[pi, qrows, :] = jnp.where(lo, ms[0], ms[1])
            lst[pi, qrows, :] = jnp.where(lo, ls[0], ls[1])
            return carry

        lax.fori_loop(0, n_blk, body, 0, unroll=unroll)

    n_pat = len(A_DILATIONS)
    cr = 256
    for c in range(span // cr):
        rs = slice(c * cr, (c + 1) * cr)
        m_p = [mst[pi, rs, :] for pi in range(n_pat)]
        m = functools.reduce(jnp.maximum, m_p)
        w = [jnp.exp2(mp - m) for mp in m_p]
        num = sum(w[pi] * acc[pi, rs, :] for pi in range(n_pat))
        den = sum(w[pi] * lst[pi, rs, :] for pi in range(n_pat))
        o_ref[rs, :] = (num / den).astype(o_ref.dtype)


def _attention(qa, ka, va, batch, seq, unroll=8):
    t = qa.shape[0]
    nt = seq // A_SPAN
    n_slab = A_WIDTH // LANES
    n_pat = len(A_DILATIONS)
    blk = pl.BlockSpec((A_SPAN, LANES), lambda b, s, i: (b * nt + i, s))
    return pl.pallas_call(
        functools.partial(_attn_kernel, unroll=unroll),
        grid=(batch, n_slab, nt),
        in_specs=[blk, blk, blk],
        out_specs=blk,
        out_shape=jax.ShapeDtypeStruct((t, A_WIDTH), BF16),
        scratch_shapes=[pltpu.VMEM((2 * A_SPAN, LANES), F32), pltpu.VMEM((2 * A_SPAN, LANES), F32),
                        pltpu.VMEM((n_pat, A_SPAN, LANES), F32), pltpu.VMEM((n_pat, A_SPAN, LANES), F32),
                        pltpu.VMEM((n_pat, A_SPAN, LANES), F32), pltpu.VMEM((2, A_BLOCK, 2 * A_BLOCK), F32)],
        compiler_params=_cparams("arbitrary", "arbitrary", "arbitrary"),
        name="dilated_attention",
    )(qa, ka, va)


def _retention_tables(chunk):
    log_gamma = jnp.log1p(-jnp.exp2(-5.0 - jnp.arange(R_HEADS, dtype=F32)))
    idx = jnp.arange(chunk, dtype=F32)
    diff = idx[:, None] - idx[None, :]
    decay = jnp.where(diff >= 0, jnp.exp(log_gamma[:, None, None] * jnp.maximum(diff, 0.0)), 0.0)
    zeta = jnp.exp(log_gamma[:, None] * (chunk - 1.0 - idx))
    xi = jnp.exp(log_gamma[:, None] * (idx + 1.0))
    cdec = jnp.exp(log_gamma * chunk)

    def slab(tab):
        tab = tab.reshape(R_HEADS // 2, 2, chunk)
        return jnp.repeat(tab.transpose(0, 2, 1), R_QK_DIM, axis=2)

    cdec_t = jnp.broadcast_to(cdec[:, None, None], (R_HEADS, 1, LANES))
    return decay.astype(F32), slab(zeta).astype(F32), slab(xi).astype(F32), cdec_t.astype(F32)


def _retention_kernel(q_ref, k_ref, v_ref, g_ref, gn_ref, dec_ref, zeta_ref, xi_ref, cdec_ref, o_ref, state,
                      *, chunk):
    @pl.when(pl.program_id(1) == 0)
    def _():
        state[...] = jnp.zeros(state.shape, F32)

    rows_total = q_ref.shape[0]
    lane = lax.broadcasted_iota(jnp.int32, (chunk, LANES), 1)
    lo = lane < R_QK_DIM
    for c in range(rows_total // chunk):
        rs = slice(c * chunk, (c + 1) * chunk)
        for s in range(R_HEADS // 2):
            qs = q_ref[rs, s * LANES:(s + 1) * LANES]
            ks = k_ref[rs, s * LANES:(s + 1) * LANES]
            kz = (ks.astype(F32) * zeta_ref[s]).astype(BF16)
            for hh in range(2):
                h = 2 * s + hh
                mask = lo if hh == 0 else jnp.logical_not(lo)
                qm = jnp.where(mask, qs, jnp.zeros_like(qs))
                vh = v_ref[rs, h * R_V_DIM:(h + 1) * R_V_DIM]
                sc = lax.dot_general(qm, ks, (((1,), (1,)), ((), ())), preferred_element_type=F32)
                sc = (sc * dec_ref[h]).astype(BF16)
                y = jnp.dot(sc, vh, preferred_element_type=F32)
                qx = (qm.astype(F32) * xi_ref[s]).astype(BF16)
                st = state[h]
                y = y + jnp.dot(qx, st.astype(BF16), preferred_element_type=F32)
                kv = lax.dot_general(kz, vh, (((0,), (0,)), ((), ())), preferred_element_type=F32)
                state[h] = cdec_ref[h] * st + kv
                mu = jnp.mean(y, axis=-1, keepdims=True)
                yc = y - mu
                var = jnp.mean(yc * yc, axis=-1, keepdims=True)
                yn = yc * lax.rsqrt(var + 1e-5) * gn_ref[:, h * R_V_DIM:(h + 1) * R_V_DIM]
                g = g_ref[rs, h * R_V_DIM:(h + 1) * R_V_DIM].astype(F32)
                o_ref[rs, h * R_V_DIM:(h + 1) * R_V_DIM] = (g * _sigmoid(g) * yn).astype(o_ref.dtype)


def _retention(qr, kr, vr, gr, gn_g, batch, seq, tr, chunk):
    t = qr.shape[0]
    nt = seq // tr
    decay, zeta, xi, cdec = _retention_tables(chunk)
    row = lambda b, i: (b * nt + i, 0)
    fix2 = lambda b, i: (0, 0)
    fix3 = lambda b, i: (0, 0, 0)
    rqk = R_HEADS * R_QK_DIM
    return pl.pallas_call(
        functools.partial(_retention_kernel, chunk=chunk),
        grid=(batch, nt),
        in_specs=[pl.BlockSpec((tr, rqk), row), pl.BlockSpec((tr, rqk), row),
                  pl.BlockSpec((tr, R_WIDTH), row), pl.BlockSpec((tr, R_WIDTH), row),
                  pl.BlockSpec((1, R_WIDTH), fix2),
                  pl.BlockSpec((R_HEADS, chunk, chunk), fix3),
                  pl.BlockSpec((R_HEADS // 2, chunk, LANES), fix3),
                  pl.BlockSpec((R_HEADS // 2, chunk, LANES), fix3),
                  pl.BlockSpec((R_HEADS, 1, LANES), fix3)],
        out_specs=pl.BlockSpec((tr, R_WIDTH), row),
        out_shape=jax.ShapeDtypeStruct((t, R_WIDTH), BF16),
        scratch_shapes=[pltpu.VMEM((R_HEADS, LANES, R_V_DIM), F32)],
        compiler_params=_cparams("arbitrary", "arbitrary"),
        name="retention",
    )(qr, kr, vr, gr, gn_g.reshape(1, R_WIDTH), decay, zeta, xi, cdec)


def _ssm_matrices(a_re, a_im, b_re, b_im, c_re, c_im, log_dt):
    ell = S_CHUNK
    g_n, p_n, c_n = S_GROUPS, S_STATE, S_GROUP
    dt = jnp.exp(log_dt.astype(F32))[:, None]
    lam_re, lam_im = a_re.astype(F32), a_im.astype(F32)
    mag = jnp.exp(lam_re * dt)
    abar_re = mag * jnp.cos(lam_im * dt)
    abar_im = mag * jnp.sin(lam_im * dt)
    den = lam_re * lam_re + lam_im * lam_im
    nr, ni = abar_re - 1.0, abar_im
    f_re = ((nr * lam_re + ni * lam_im) / den)[..., None]
    f_im = ((ni * lam_re - nr * lam_im) / den)[..., None]
    br, bi = b_re.astype(F32), b_im.astype(F32)
    bb_re = f_re * br - f_im * bi
    bb_im = f_re * bi + f_im * br
    j = jnp.arange(ell + 1, dtype=F32)[:, None, None]
    pw_mag = jnp.exp(j * (lam_re * dt)[None])
    pw_ang = j * (lam_im * dt)[None]
    pw_re = pw_mag * jnp.cos(pw_ang)
    pw_im = pw_mag * jnp.sin(pw_ang)
    cr, ci = c_re.astype(F32), c_im.astype(F32)
    hi = lax.Precision.HIGHEST
    w_re = cr[None] * pw_re[:, :, None, :] - ci[None] * pw_im[:, :, None, :]
    w_im = cr[None] * pw_im[:, :, None, :] + ci[None] * pw_re[:, :, None, :]
    kern = (jnp.einsum('jgcp,gpd->jgcd', w_re[:ell], bb_re, precision=hi)
            - jnp.einsum('jgcp,gpd->jgcd', w_im[:ell], bb_im, precision=hi))
    lag_rows = kern.transpose(1, 3, 2, 0).reshape(g_n, c_n, c_n * ell)
    toep = pl.pallas_call(
        _toeplitz_kernel,
        grid=(g_n,),
        in_specs=[pl.BlockSpec((1, c_n, c_n * ell), lambda g: (g, 0, 0))],
        out_specs=pl.BlockSpec((1, c_n * ell, c_n * ell), lambda g: (g, 0, 0)),
        out_shape=jax.ShapeDtypeStruct((g_n, c_n * ell, c_n * ell), BF16),
        compiler_params=_cparams("arbitrary"),
        name="ssm_toeplitz",
    )(lag_rows)
    e_re = pw_re[ell - 1 - np.arange(ell)].transpose(1, 0, 2)[:, None]
    e_im = pw_im[ell - 1 - np.arange(ell)].transpose(1, 0, 2)[:, None]
    bbt_re = bb_re.transpose(0, 2, 1)[:, :, None, :]
    bbt_im = bb_im.transpose(0, 2, 1)[:, :, None, :]
    bz_re = (e_re * bbt_re - e_im * bbt_im).reshape(g_n, c_n * ell, p_n)
    bz_im = (e_re * bbt_im + e_im * bbt_re).reshape(g_n, c_n * ell, p_n)
    zeros = jnp.zeros_like(bz_re[0::2])
    top = jnp.concatenate([bz_re[0::2], zeros, bz_im[0::2], zeros], axis=-1)
    bot = jnp.concatenate([zeros, bz_re[1::2], zeros, bz_im[1::2]], axis=-1)
    bz = jnp.concatenate([top, bot], axis=1)
    cz_re = w_re[1:].transpose(1, 3, 2, 0).reshape(g_n, p_n, c_n * ell)
    cz_im = -w_im[1:].transpose(1, 3, 2, 0).reshape(g_n, p_n, c_n * ell)
    zc = jnp.zeros_like(cz_re[0::2])
    cz = jnp.concatenate([
        jnp.concatenate([cz_re[0::2], zc], axis=-1),
        jnp.concatenate([zc, cz_re[1::2]], axis=-1),
        jnp.concatenate([cz_im[0::2], zc], axis=-1),
        jnp.concatenate([zc, cz_im[1::2]], axis=-1)], axis=1)
    al_re = pw_re[ell].reshape(S_PAIRS, 2 * p_n)
    al_im = pw_im[ell].reshape(S_PAIRS, 2 * p_n)
    a_l = jnp.stack([al_re, al_im], axis=0)
    return toep, bz.astype(BF16), cz.astype(BF16), a_l.astype(F32)


def _toeplitz_kernel(k_ref, m_ref):
    ell = S_CHUNK
    width = k_ref.shape[2]
    s_i = lax.broadcasted_iota(jnp.int32, (ell, width), 0)
    t_i = lax.broadcasted_iota(jnp.int32, (ell, width), 1) % ell
    for c in range(k_ref.shape[1]):
        row = jnp.broadcast_to(k_ref[0, c:c + 1, :], (ell, width))
        shifted = pltpu.roll(row, 0, 1, stride=1, stride_axis=0)
        m_ref[0, c * ell:(c + 1) * ell, :] = jnp.where(t_i >= s_i, shifted, 0.0).astype(m_ref.dtype)


def _ssm_state_kernel(u_ref, bz_ref, s_ref):
    s_ref[...] = jnp.dot(u_ref[...], bz_ref[0], preferred_element_type=F32)


def _ssm_scan_kernel(s_ref, al_ref, h_ref, *, batch, n_chunks):
    n_blk = 2 * S_PAIRS
    a_re = [al_ref[0, k:k + 1, :] for k in range(S_PAIRS)]
    a_im = [al_ref[1, k:k + 1, :] for k in range(S_PAIRS)]

    def body(n, carry):
        new = []
        for b in range(batch):
            row = b * n_chunks + n
            h = carry[b * n_blk:(b + 1) * n_blk]
            s_row = s_ref[pl.ds(row, 1), :]
            h_ref[pl.ds(row, 1), :] = jnp.concatenate(h, axis=1)
            for k in range(S_PAIRS):
                hr, hi = h[2 * k], h[2 * k + 1]
                sr = s_row[:, (2 * k) * LANES:(2 * k + 1) * LANES]
                si = s_row[:, (2 * k + 1) * LANES:(2 * k + 2) * LANES]
                new.append(a_re[k] * hr - a_im[k] * hi + sr)
                new.append(a_re[k] * hi + a_im[k] * hr + si)
        return tuple(new)

    init = tuple(jnp.zeros((1, LANES), F32) for _ in range(batch * n_blk))
    lax.fori_loop(0, n_chunks, body, init)


def _ssm_out_kernel(u_ref, toep_ref, h_ref, cz_ref, y_ref):
    half = S_CHUNK * S_GROUP
    cross = jnp.dot(h_ref[...].astype(BF16), cz_ref[0], preferred_element_type=F32)
    for g in range(2):
        sl = slice(g * half, (g + 1) * half)
        y = jnp.dot(u_ref[:, sl], toep_ref[g], preferred_element_type=F32) + cross[:, sl]
        y_ref[:, sl] = y.astype(y_ref.dtype)


def _ssm_conv(us, mats, batch, seq):
    toep, bz, cz, a_l = mats
    t = us.shape[0]
    ell = S_CHUNK
    n_chunks = seq // ell
    n_all = batch * n_chunks
    pair_w = 2 * ell * S_GROUP
    u_t = us.astype(BF16).reshape(n_all, ell, S_CHANNELS).transpose(0, 2, 1).reshape(n_all, S_PAIRS * pair_w)
    st_w = 4 * S_STATE
    s_all = pl.pallas_call(
        _ssm_state_kernel,
        grid=(S_PAIRS,),
        in_specs=[pl.BlockSpec((n_all, pair_w), lambda k: (0, k)),
                  pl.BlockSpec((1, pair_w, st_w), lambda k: (k, 0, 0))],
        out_specs=pl.BlockSpec((n_all, st_w), lambda k: (0, k)),
        out_shape=jax.ShapeDtypeStruct((n_all, S_PAIRS * st_w), F32),
        compiler_params=_cparams("arbitrary"),
        name="ssm_chunk_state",
    )(u_t, bz)
    h_prev = pl.pallas_call(
        functools.partial(_ssm_scan_kernel, batch=batch, n_chunks=n_chunks),
        out_shape=jax.ShapeDtypeStruct((n_all, S_PAIRS * st_w), F32),
        compiler_params=pltpu.CompilerParams(vmem_limit_bytes=VMEM_LIMIT),
        name="ssm_chunk_scan",
    )(s_all, a_l)
    y_t = pl.pallas_call(
        _ssm_out_kernel,
        grid=(S_PAIRS,),
        in_specs=[pl.BlockSpec((n_all, pair_w), lambda k: (0, k)),
                  pl.BlockSpec((2, pair_w // 2, pair_w // 2), lambda k: (k, 0, 0)),
                  pl.BlockSpec((n_all, st_w), lambda k: (0, k)),
                  pl.BlockSpec((1, st_w, pair_w), lambda k: (k, 0, 0))],
        out_specs=pl.BlockSpec((n_all, pair_w), lambda k: (0, k)),
        out_shape=jax.ShapeDtypeStruct((n_all, S_PAIRS * pair_w), BF16),
        compiler_params=_cparams("arbitrary"),
        name="ssm_chunk_out",
    )(u_t, toep, h_prev, cz)
    return y_t.reshape(n_all, S_CHANNELS, ell).transpose(0, 2, 1).reshape(t, S_CHANNELS)


def _out_proj_kernel(*refs, with_router):
    if with_router:
        (h_ref, ya_ref, yr_ref, ys_ref, us_ref, d_ref, gw_ref, gb_ref, w_ref, n2_ref, rt_ref,
         h1_ref, xn_ref, lg_ref) = refs
    else:
        (h_ref, ya_ref, yr_ref, ys_ref, us_ref, d_ref, gw_ref, gb_ref, w_ref, n2_ref,
         h1_ref, xn_ref) = refs
    y = ys_ref[...].astype(F32) + d_ref[...] * us_ref[...]
    z = 0.5 * y * (1.0 + jnp.tanh(math.sqrt(2.0 / math.pi) * (y + 0.044715 * (y * y * y))))
    gate = jnp.dot(z.astype(BF16), gw_ref[...], preferred_element_type=F32) + gb_ref[...]
    yc = (z * _sigmoid(gate)).astype(BF16)
    c1 = A_WIDTH
    c2 = A_WIDTH + R_WIDTH
    acc = jnp.dot(ya_ref[...], w_ref[0:c1, :], preferred_element_type=F32)
    acc = acc + jnp.dot(yr_ref[...], w_ref[c1:c2, :], preferred_element_type=F32)
    acc = acc + jnp.dot(yc, w_ref[c2:, :], preferred_element_type=F32)
    h1 = h_ref[...] + acc
    h1_ref[...] = h1
    ms = jnp.mean(h1 * h1, axis=-1, keepdims=True)
    xn = h1 * lax.rsqrt(ms + EPS) * n2_ref[...]
    if not with_router:
        xn_ref[...] = xn.astype(xn_ref.dtype)
    else:
        n_slab = xn.shape[1] // LANES
        for c in range(n_slab):
            xn_ref[pl.ds(c, xn.shape[0], stride=n_slab), :] = xn[:, c * LANES:(c + 1) * LANES]
        x_hi = xn.astype(BF16)
        x_lo = (xn - x_hi.astype(F32)).astype(BF16)
        both = jnp.dot(x_hi, rt_ref[...], preferred_element_type=F32)
        lg_ref[...] = (both[:, :LANES] + both[:, LANES:]
                       + jnp.dot(x_lo, rt_ref[:, :LANES], preferred_element_type=F32))


def _out_proj(h2d, ya, yr, ys, us, d_skip, glu_w, glu_b, w_out, norm2_g, router, tm):
    t, d = h2d.shape
    with_router = router is not None
    row = lambda i: (i, 0)
    fixed = lambda i: (0, 0)
    in_specs = [pl.BlockSpec((tm, d), row), pl.BlockSpec((tm, A_WIDTH), row), pl.BlockSpec((tm, R_WIDTH), row),
                pl.BlockSpec((tm, S_CHANNELS), row), pl.BlockSpec((tm, S_CHANNELS), row),
                pl.BlockSpec((1, S_CHANNELS), fixed), pl.BlockSpec((S_CHANNELS, S_CHANNELS), fixed),
                pl.BlockSpec((1, S_CHANNELS), fixed), pl.BlockSpec(w_out.shape, fixed), pl.BlockSpec((1, d), fixed)]
    args = [h2d, ya, yr, ys, us, d_skip.reshape(1, -1), glu_w, glu_b.reshape(1, -1), w_out, norm2_g.reshape(1, d)]
    out_shapes = [jax.ShapeDtypeStruct((t, d), F32), jax.ShapeDtypeStruct((t, d), BF16)]
    out_specs = [pl.BlockSpec((tm, d), row), pl.BlockSpec((tm, d), row)]
    if with_router:
        n_slab = d // LANES
        out_shapes[1] = jax.ShapeDtypeStruct((t * n_slab, LANES), F32)
        out_specs[1] = pl.BlockSpec((tm * n_slab, LANES), row)
        rt = jnp.pad(router.astype(F32), ((0, 0), (0, LANES - router.shape[1])))
        rt_hi = rt.astype(BF16)
        rt_lo = (rt - rt_hi.astype(F32)).astype(BF16)
        in_specs.append(pl.BlockSpec((d, 2 * LANES), fixed))
        args.append(jnp.concatenate([rt_hi, rt_lo], axis=1))
        out_shapes.append(jax.ShapeDtypeStruct((t, LANES), F32))
        out_specs.append(pl.BlockSpec((tm, LANES), row))
    return pl.pallas_call(
        functools.partial(_out_proj_kernel, with_router=with_router),
        grid=(t // tm,),
        in_specs=in_specs,
        out_specs=out_specs,
        out_shape=out_shapes,
        compiler_params=_cparams("arbitrary"),
        name="out_proj",
    )(*args)


def _swiglu_chunk(x, wg, wu, wd):
    hg = jnp.dot(x, wg, preferred_element_type=F32)
    hu = jnp.dot(x, wu, preferred_element_type=F32)
    a = (hg * _sigmoid(hg) * hu).astype(BF16)
    return jnp.dot(a, wd, preferred_element_type=F32)


def _ffn_kernel(x_ref, h_ref, wg_ref, wu_ref, wd_ref, o_ref):
    @pl.when(pl.program_id(1) == 0)
    def _():
        o_ref[...] = h_ref[...]

    o_ref[...] += _swiglu_chunk(x_ref[...], wg_ref[...], wu_ref[...], wd_ref[...])


def _ffn(xn, h1, wg, wu, wd, tm, fc):
    t, d = xn.shape
    f = wg.shape[1]
    return pl.pallas_call(
        _ffn_kernel,
        grid=(t // tm, f // fc),
        in_specs=[pl.BlockSpec((tm, d), lambda i, j: (i, 0)), pl.BlockSpec((tm, d), lambda i, j: (i, 0)),
                  pl.BlockSpec((d, fc), lambda i, j: (0, j)), pl.BlockSpec((d, fc), lambda i, j: (0, j)),
                  pl.BlockSpec((fc, d), lambda i, j: (j, 0))],
        out_specs=pl.BlockSpec((tm, d), lambda i, j: (i, 0)),
        out_shape=jax.ShapeDtypeStruct((t, d), F32),
        compiler_params=_cparams("arbitrary", "arbitrary"),
        name="swiglu_ffn",
    )(xn, h1, wg, wu, wd)


def _route_kernel(lg_ref, tri_ref, gate_ref, idx_ref, cnt_ref, carry):
    @pl.when(pl.program_id(0) == 0)
    def _():
        carry[...] = jnp.zeros(carry.shape, F32)

    lg = lg_ref[...]
    lane = lax.broadcasted_iota(jnp.int32, lg.shape, 1)
    valid = lane < N_EXPERTS
    mx = jnp.max(jnp.where(valid, lg, NEG_BIG), axis=-1, keepdims=True)
    ex = jnp.where(valid, jnp.exp(lg - mx), 0.0)
    probs = ex / jnp.sum(ex, axis=-1, keepdims=True)
    p1 = jnp.max(probs, axis=-1, keepdims=True)
    e1 = jnp.min(jnp.where(valid & (probs == p1), lane, LANES), axis=-1, keepdims=True)
    rest = jnp.where(valid & (lane != e1), probs, -1.0)
    p2 = jnp.max(rest, axis=-1, keepdims=True)
    e2 = jnp.min(jnp.where(rest == p2, lane, LANES), axis=-1, keepdims=True)
    den = p1 + p2
    oh1 = lane == e1
    oh2 = lane == e2
    oh = jnp.where(oh1 | oh2, 1.0, 0.0)
    cum = jnp.dot(tri_ref[...], oh.astype(BF16), preferred_element_type=F32)
    excl = cum - oh + carry[...]
    r1 = jnp.sum(jnp.where(oh1, excl, 0.0), axis=-1, keepdims=True)
    r2 = jnp.sum(jnp.where(oh2, excl, 0.0), axis=-1, keepdims=True)
    tot = carry[...] + cum[cum.shape[0] - 1:cum.shape[0], :]
    carry[...] = tot
    cnt_ref[...] = tot.astype(jnp.int32)
    gate_ref[...] = jnp.where(lane == 0, p1 / den, jnp.where(lane == 1, p2 / den, 0.0))
    idx_ref[...] = jnp.where(lane == 0, e1, jnp.where(lane == 1, e2, jnp.where(
        lane == 2, r1.astype(jnp.int32), jnp.where(lane == 3, r2.astype(jnp.int32), 0))))


def _route(logits, tm):
    t = logits.shape[0]
    tri = jnp.asarray(np.tril(np.ones((tm, tm), np.float32)), BF16)
    row = lambda i: (i, 0)
    return pl.pallas_call(
        _route_kernel,
        grid=(t // tm,),
        in_specs=[pl.BlockSpec((tm, LANES), row), pl.BlockSpec((tm, tm), lambda i: (0, 0))],
        out_specs=[pl.BlockSpec((tm, LANES), row), pl.BlockSpec((tm, LANES), row),
                   pl.BlockSpec((1, LANES), lambda i: (0, 0))],
        out_shape=[jax.ShapeDtypeStruct((t, LANES), F32), jax.ShapeDtypeStruct((t, LANES), jnp.int32),
                   jax.ShapeDtypeStruct((1, LANES), jnp.int32)],
        scratch_shapes=[pltpu.VMEM((1, LANES), F32)],
        compiler_params=_cparams("arbitrary"),
        name="moe_route",
    )(logits, tri)


DMA_UNROLL = 8
SLABS = 8


def _slab_rows(row):
    return pl.ds(pl.multiple_of(row * SLABS, SLABS), SLABS)


def _dispatch_kernel(tok_ref, x_hbm, o_ref, buf, sem, *, tm):
    i = pl.program_id(0)
    n = pl.num_programs(0)

    def issue(step, slot_buf):
        def body(jb, c):
            for u in range(DMA_UNROLL):
                j = jb * DMA_UNROLL + u
                tok = tok_ref[step * tm + j]
                pltpu.make_async_copy(x_hbm.at[_slab_rows(tok)], buf.at[slot_buf, _slab_rows(j)],
                                      sem.at[slot_buf]).start(priority=u % 2)
            return c
        lax.fori_loop(0, tm // DMA_UNROLL, body, 0)

    @pl.when(i == 0)
    def _():
        issue(0, 0)

    @pl.when(i + 1 < n)
    def _():
        issue(i + 1, (i + 1) % 2)

    cur = i % 2
    pltpu.make_async_copy(x_hbm.at[pl.ds(0, tm * SLABS)], buf.at[cur], sem.at[cur]).wait()
    o_ref[...] = buf[cur]


def _dispatch(xn, tok_sorted, tm):
    n_rows = tok_sorted.shape[0]
    return pl.pallas_call(
        functools.partial(_dispatch_kernel, tm=tm),
        grid_spec=pltpu.PrefetchScalarGridSpec(
            num_scalar_prefetch=1,
            grid=(n_rows // tm,),
            in_specs=[pl.BlockSpec(memory_space=pl.ANY)],
            out_specs=pl.BlockSpec((tm * SLABS, LANES), lambda i, tok: (i, 0)),
            scratch_shapes=[pltpu.VMEM((2, tm * SLABS, LANES), xn.dtype), pltpu.SemaphoreType.DMA((2,))]),
        out_shape=jax.ShapeDtypeStruct((n_rows * SLABS, LANES), xn.dtype),
        compiler_params=_cparams("arbitrary"),
        name="moe_dispatch",
    )(tok_sorted, xn)


def _moe_kernel(te_ref, nu_ref, x_ref, wg_ref, wu_ref, wd_ref, o_ref, xb, acc):
    i = pl.program_id(0)
    f = pl.program_id(1)
    tm = xb.shape[0]

    @pl.when(i < nu_ref[0])
    def _():
        @pl.when(f == 0)
        def _():
            for c in range(SLABS):
                xb[:, c * LANES:(c + 1) * LANES] = x_ref[pl.ds(c, tm, stride=SLABS), :].astype(BF16)
            acc[...] = jnp.zeros(acc.shape, F32)

        acc[...] += _swiglu_chunk(xb[...], wg_ref[0], wu_ref[0], wd_ref[0])

        @pl.when(f == pl.num_programs(1) - 1)
        def _():
            for c in range(SLABS):
                o_ref[pl.ds(c, tm, stride=SLABS), :] = acc[:, c * LANES:(c + 1) * LANES]

    @pl.when(jnp.logical_and(i >= nu_ref[0], f == 0))
    def _():
        o_ref[...] = jnp.zeros(o_ref.shape, F32)


def _moe_experts(xs, tile_expert, n_used, wg, wu, wd, tm, fc):
    d = wg.shape[1]
    f = wg.shape[2]
    nf = f // fc
    n_tiles = xs.shape[0] // (tm * SLABS)

    def x_map(i, j, te, nu):
        return (jnp.minimum(i, nu[0] - 1), 0)

    def f_eff(i, j, nu):
        return jnp.where(i < nu[0], j, nf - 1)

    return pl.pallas_call(
        _moe_kernel,
        grid_spec=pltpu.PrefetchScalarGridSpec(
            num_scalar_prefetch=2,
            grid=(n_tiles, nf),
            in_specs=[pl.BlockSpec((tm * SLABS, LANES), x_map),
                      pl.BlockSpec((1, d, fc), lambda i, j, te, nu: (te[i], 0, f_eff(i, j, nu))),
                      pl.BlockSpec((1, d, fc), lambda i, j, te, nu: (te[i], 0, f_eff(i, j, nu))),
                      pl.BlockSpec((1, fc, d), lambda i, j, te, nu: (te[i], f_eff(i, j, nu), 0))],
            out_specs=pl.BlockSpec((tm * SLABS, LANES), lambda i, j, te, nu: (i, 0)),
            scratch_shapes=[pltpu.VMEM((tm, d), BF16), pltpu.VMEM((tm, d), F32)]),
        out_shape=jax.ShapeDtypeStruct(xs.shape, F32),
        compiler_params=_cparams("arbitrary", "arbitrary"),
        name="moe_experts",
    )(tile_expert, n_used, xs, wg, wu, wd)


def _combine_kernel(pos_ref, h_ref, gate_ref, ys_hbm, o_ref, buf, sem, *, tc):
    i = pl.program_id(0)
    n = pl.num_programs(0)

    def issue(step, slot_buf):
        def body(jb, c):
            for u in range(DMA_UNROLL):
                j = jb * DMA_UNROLL + u
                for k in range(2):
                    src = pos_ref[2 * (step * tc + j) + k]
                    pltpu.make_async_copy(ys_hbm.at[_slab_rows(src)], buf.at[slot_buf, k, _slab_rows(j)],
                                          sem.at[slot_buf]).start(priority=k)
            return c
        lax.fori_loop(0, tc // DMA_UNROLL, body, 0)

    @pl.when(i == 0)
    def _():
        issue(0, 0)

    @pl.when(i + 1 < n)
    def _():
        issue(i + 1, (i + 1) % 2)

    cur = i % 2
    for k in range(2):
        pltpu.make_async_copy(ys_hbm.at[pl.ds(0, tc * SLABS)], buf.at[cur, k], sem.at[cur]).wait()
    g = gate_ref[...]
    for c in range(SLABS):
        cols = slice(c * LANES, (c + 1) * LANES)
        o_ref[:, cols] = (h_ref[:, cols] + g[:, 0:1] * buf[cur, 0, pl.ds(c, tc, stride=SLABS), :]
                          + g[:, 1:2] * buf[cur, 1, pl.ds(c, tc, stride=SLABS), :])


def _combine(h1, gates, ys, pos_flat, tc):
    t, d = h1.shape
    row = lambda i, p: (i, 0)
    return pl.pallas_call(
        functools.partial(_combine_kernel, tc=tc),
        grid_spec=pltpu.PrefetchScalarGridSpec(
            num_scalar_prefetch=1,
            grid=(t // tc,),
            in_specs=[pl.BlockSpec((tc, d), row), pl.BlockSpec((tc, LANES), row),
                      pl.BlockSpec(memory_space=pl.ANY)],
            out_specs=pl.BlockSpec((tc, d), row),
            scratch_shapes=[pltpu.VMEM((2, 2, tc * SLABS, LANES), F32), pltpu.SemaphoreType.DMA((2,))]),
        out_shape=jax.ShapeDtypeStruct((t, d), F32),
        compiler_params=_cparams("arbitrary"),
        name="moe_combine",
    )(pos_flat, h1, gates, ys)


def _moe(xn, h1, logits, wg, wu, wd, tm, fc):
    t = h1.shape[0]
    gates, idx, counts = _route(logits, 512)
    counts = counts[0, :N_EXPERTS]
    tiles_per = (counts + tm - 1) // tm
    tile_end = jnp.cumsum(tiles_per)
    group_start = (tile_end - tiles_per) * tm
    pos = group_start[idx[:, 0:2]] + idx[:, 2:4]
    pos_flat = pos.reshape(-1).astype(jnp.int32)
    n_tiles = (2 * t) // tm + N_EXPERTS
    n_used = tile_end[-1].astype(jnp.int32).reshape(1)
    tile_ids = jnp.minimum(jnp.arange(n_tiles, dtype=jnp.int32), n_used[0] - 1)
    tile_expert = jnp.sum(tile_ids[:, None] >= tile_end[None, :], axis=1).astype(jnp.int32)
    n_holes = n_tiles * tm - 2 * t
    hole_cnt = tiles_per * tm - counts
    hole_end = jnp.cumsum(hole_cnt)
    j = jnp.arange(n_holes, dtype=jnp.int32)
    he = jnp.sum(j[:, None] >= hole_end[None, :], axis=1)
    onehot = he[:, None] == jnp.arange(N_EXPERTS, dtype=jnp.int32)[None, :]
    in_group = jnp.sum(jnp.where(onehot, (group_start + counts - (hole_end - hole_cnt))[None, :], 0), axis=1) + j
    in_tail = n_used[0] * tm + (j - hole_end[-1])
    hole_pos = jnp.where(he < N_EXPERTS, in_group, in_tail).astype(jnp.int32)
    keys = jnp.concatenate([pos_flat, hole_pos])
    vals = jnp.concatenate([jnp.arange(2 * t, dtype=jnp.int32) // 2, jnp.zeros((n_holes,), jnp.int32)])
    _, tok_sorted = lax.sort((keys, vals), num_keys=1)
    xs = _dispatch(xn, tok_sorted, tm)
    ys = _moe_experts(xs, tile_expert, n_used, wg, wu, wd, tm, fc)
    return _combine(h1, gates, ys, pos_flat, 256)


def _trunk(x, norm1_g, w_in, q_norm_g, k_norm_g, ret_gn_g, ssm_a_re, ssm_a_im, ssm_b_re, ssm_b_im,
           ssm_c_re, ssm_c_im, ssm_d, ssm_log_dt, ssm_glu_w, ssm_glu_b, w_out, norm2_g,
           ffn_w_gate, ffn_w_up, ffn_w_down, moe_router, moe_w_gate, moe_w_up, moe_w_down,
           *, tm=512, tr=512, r_chunk=128, ffn_fc=1408, moe_tm=512, moe_fc=1792):
    batch, seq, d = x.shape
    depth = norm1_g.shape[0]
    t = batch * seq
    tabs = _rope_tables(seq, ROPE_DIM, ROPE_THETA) + _rope_tables(seq, R_QK_DIM, R_ROPE_THETA)
    h = x.reshape(t, d).astype(F32)
    for layer in range(depth):
        qa, ka, va, qr, kr, vr, gr, us = _in_proj(
            h, seq, norm1_g[layer], w_in[layer].astype(BF16), q_norm_g[layer], k_norm_g[layer], tabs, tm)
        ya = _attention(qa, ka, va, batch, seq)
        yr = _retention(qr, kr, vr, gr, ret_gn_g[layer], batch, seq, tr, r_chunk)
        mats = _ssm_matrices(ssm_a_re[layer], ssm_a_im[layer], ssm_b_re[layer], ssm_b_im[layer],
                             ssm_c_re[layer], ssm_c_im[layer], ssm_log_dt[layer])
        ys = _ssm_conv(us, mats, batch, seq)
        i = layer // 2
        router = moe_router[i] if layer % 2 == 1 else None
        outs = _out_proj(h, ya, yr, ys, us, ssm_d[layer], ssm_glu_w[layer].astype(BF16), ssm_glu_b[layer],
                         w_out[layer].astype(BF16), norm2_g[layer], router, tm)
        if layer % 2 == 0:
            h1, xn = outs
            h = _ffn(xn, h1, ffn_w_gate[i].astype(BF16), ffn_w_up[i].astype(BF16), ffn_w_down[i].astype(BF16),
                     2 * tm, ffn_fc)
        else:
            h1, xn, logits = outs
            h = _moe(xn, h1, logits, moe_w_gate[i].astype(BF16), moe_w_up[i].astype(BF16),
                     moe_w_down[i].astype(BF16), moe_tm, moe_fc)
    return h.reshape(batch, seq, d).astype(x.dtype)


def kernel(x, norm1_g, w_in, q_norm_g, k_norm_g, ret_gn_g, ssm_a_re, ssm_a_im, ssm_b_re, ssm_b_im, ssm_c_re,
           ssm_c_im, ssm_d, ssm_log_dt, ssm_glu_w, ssm_glu_b, w_out, norm2_g, ffn_w_gate, ffn_w_up, ffn_w_down,
           moe_router, moe_w_gate, moe_w_up, moe_w_down):
    return _trunk(x, norm1_g, w_in, q_norm_g, k_norm_g, ret_gn_g, ssm_a_re, ssm_a_im, ssm_b_re, ssm_b_im,
                  ssm_c_re, ssm_c_im, ssm_d, ssm_log_dt, ssm_glu_w, ssm_glu_b, w_out, norm2_g,
                  ffn_w_gate, ffn_w_up, ffn_w_down, moe_router, moe_w_gate, moe_w_up, moe_w_down)
```

```python
import functools
import math

import jax
import jax.numpy as jnp
import numpy as np
from jax import lax
from jax.experimental import pallas as pl
from jax.experimental.pallas import tpu as pltpu

F32 = jnp.float32
BF16 = jnp.bfloat16

LANES = 128
EPS = 1e-6
HEAD_DIM = 64
A_HEADS = 4
A_BLOCK = 128
A_DILATIONS = (1, 4, 16)
A_SPAN = A_BLOCK * max(A_DILATIONS)
ROPE_THETA = 500000.0
ROPE_DIM = HEAD_DIM // 4
R_HEADS = 4
R_QK_DIM = 64
R_V_DIM = 128
R_ROPE_THETA = 10000.0
S_CHANNELS = 256
S_GROUP = 16
S_GROUPS = S_CHANNELS // S_GROUP
S_STATE = 64
S_CHUNK = 32
S_PAIRS = S_GROUPS // 2
A_WIDTH = A_HEADS * HEAD_DIM
R_WIDTH = R_HEADS * R_V_DIM
N_EXPERTS = 8
NEG_BIG = -1e30

VMEM_LIMIT = 56 * 1024 * 1024


def _cparams(*sem):
    return pltpu.CompilerParams(dimension_semantics=sem, vmem_limit_bytes=VMEM_LIMIT)


def _sigmoid(x):
    return 1.0 / (1.0 + jnp.exp(-x))


def _rope_tables(seq, rot_dim, theta):
    half = rot_dim // 2
    inv = jnp.power(theta, -jnp.arange(half, dtype=F32) * 2.0 / rot_dim)
    ang = jnp.arange(seq, dtype=jnp.int32).astype(F32)[:, None] * inv[None, :]
    d = np.arange(LANES) % HEAD_DIM
    idx = jnp.asarray(d % half)
    cos = jnp.cos(ang)[:, idx]
    sin = jnp.sin(ang)[:, idx]
    in_rot = jnp.asarray(d < rot_dim)[None, :]
    first = jnp.asarray(d < half)[None, :]
    cos_t = jnp.where(in_rot, cos, 1.0)
    sin_t = jnp.where(in_rot, jnp.where(first, -sin, sin), 0.0)
    return cos_t.astype(F32), sin_t.astype(F32)


def _rope_slab(x, cos_t, sin_t, half):
    lane = lax.broadcasted_iota(jnp.int32, x.shape, 1)
    fwd = pltpu.roll(x, LANES - half, 1)
    bwd = pltpu.roll(x, half, 1)
    partner = jnp.where((lane % (2 * half)) < half, fwd, bwd)
    return x * cos_t + partner * sin_t


def _head_rms_slab(x, g):
    lane = lax.broadcasted_iota(jnp.int32, x.shape, 1)
    lo = lane < HEAD_DIM
    x2 = x * x
    s0 = jnp.sum(jnp.where(lo, x2, 0.0), axis=-1, keepdims=True)
    s1 = jnp.sum(jnp.where(lo, 0.0, x2), axis=-1, keepdims=True)
    ms = jnp.where(lo, s0, s1) * (1.0 / HEAD_DIM)
    return x * lax.rsqrt(ms + EPS) * g


def _in_proj_kernel(x_ref, g_ref, w_ref, qg_ref, kg_ref, ca_ref, sa_ref, cr_ref, sr_ref,
                    qa_ref, ka_ref, va_ref, qr_ref, kr_ref, vr_ref, gr_ref, us_ref):
    x = x_ref[...]
    ms = jnp.mean(x * x, axis=-1, keepdims=True)
    xn = (x * lax.rsqrt(ms + EPS) * g_ref[...]).astype(BF16)

    def proj(c0, n):
        return jnp.dot(xn, w_ref[:, c0:c0 + n], preferred_element_type=F32)

    ca, sa, cr, sr = ca_ref[...], sa_ref[...], cr_ref[...], sr_ref[...]
    qa = proj(0, A_WIDTH)
    ka = proj(A_WIDTH, A_WIDTH)
    for s in range(A_WIDTH // LANES):
        sl = slice(s * LANES, (s + 1) * LANES)
        qn = _rope_slab(_head_rms_slab(qa[:, sl], qg_ref[...]), ca, sa, ROPE_DIM // 2)
        qa_ref[:, sl] = qn * (math.log2(math.e) * HEAD_DIM ** -0.5)
        ka_ref[:, sl] = _rope_slab(_head_rms_slab(ka[:, sl], kg_ref[...]), ca, sa, ROPE_DIM // 2)
    va_ref[...] = proj(2 * A_WIDTH, A_WIDTH)
    c0 = 3 * A_WIDTH
    rqk = R_HEADS * R_QK_DIM
    qr = proj(c0, rqk)
    kr = proj(c0 + rqk, rqk)
    for s in range(rqk // LANES):
        sl = slice(s * LANES, (s + 1) * LANES)
        qr_ref[:, sl] = _rope_slab(qr[:, sl], cr, sr, R_QK_DIM // 2).astype(BF16)
        kr_ref[:, sl] = (_rope_slab(kr[:, sl], cr, sr, R_QK_DIM // 2) * (R_QK_DIM ** -0.5)).astype(BF16)
    c0 += 2 * rqk
    vr_ref[...] = proj(c0, R_WIDTH).astype(BF16)
    gr_ref[...] = proj(c0 + R_WIDTH, R_WIDTH).astype(BF16)
    us_ref[...] = proj(c0 + 2 * R_WIDTH, S_CHANNELS)


def _in_proj(h2d, seq, norm_g, w_in, q_g, k_g, tabs, tm):
    t, d = h2d.shape
    n_cols = w_in.shape[1]
    nt_seq = seq // tm
    row = lambda i: (i, 0)
    fixed = lambda i: (0, 0)
    tab = lambda i: (i % nt_seq, 0)
    out_shapes = (
        jax.ShapeDtypeStruct((t, A_WIDTH), F32), jax.ShapeDtypeStruct((t, A_WIDTH), F32),
        jax.ShapeDtypeStruct((t, A_WIDTH), F32),
        jax.ShapeDtypeStruct((t, R_HEADS * R_QK_DIM), BF16), jax.ShapeDtypeStruct((t, R_HEADS * R_QK_DIM), BF16),
        jax.ShapeDtypeStruct((t, R_WIDTH), BF16), jax.ShapeDtypeStruct((t, R_WIDTH), BF16),
        jax.ShapeDtypeStruct((t, S_CHANNELS), F32))
    return pl.pallas_call(
        _in_proj_kernel,
        grid=(t // tm,),
        in_specs=[pl.BlockSpec((tm, d), row), pl.BlockSpec((1, d), fixed), pl.BlockSpec((d, n_cols), fixed),
                  pl.BlockSpec((1, LANES), fixed), pl.BlockSpec((1, LANES), fixed)]
                 + [pl.BlockSpec((tm, LANES), tab)] * 4,
        out_specs=[pl.BlockSpec((tm, s.shape[1]), row) for s in out_shapes],
        out_shape=out_shapes,
        compiler_params=_cparams("arbitrary"),
        name="in_proj",
    )(h2d, norm_g.reshape(1, d), w_in, jnp.tile(q_g, 2).reshape(1, LANES), jnp.tile(k_g, 2).reshape(1, LANES), *tabs)


def _attn_kernel(q_ref, k_ref, v_ref, o_ref, kbuf, vbuf, acc, mst, lst, bias, *, unroll):
    i = pl.program_id(2)
    span = A_SPAN

    @pl.when(i == 0)
    def _():
        kbuf[0:span, :] = jnp.zeros((span, LANES), F32)
        vbuf[0:span, :] = jnp.zeros((span, LANES), F32)

    @pl.when(i > 0)
    def _():
        kbuf[0:span, :] = kbuf[span:2 * span, :]
        vbuf[0:span, :] = vbuf[span:2 * span, :]

    kbuf[span:2 * span, :] = k_ref[...]
    vbuf[span:2 * span, :] = v_ref[...]

    qi = lax.broadcasted_iota(jnp.int32, (A_BLOCK, 2 * A_BLOCK), 0)
    kj = lax.broadcasted_iota(jnp.int32, (A_BLOCK, 2 * A_BLOCK), 1)
    bias[0] = jnp.where(kj >= qi, jnp.where(kj <= qi + A_BLOCK, 0.0, NEG_BIG), NEG_BIG)
    bias[1] = jnp.where(kj >= jnp.maximum(qi, A_BLOCK), jnp.where(kj <= qi + A_BLOCK, 0.0, NEG_BIG), NEG_BIG)
    lane = lax.broadcasted_iota(jnp.int32, (A_BLOCK, LANES), 1)
    lo = lane < HEAD_DIM

    def rows(start, n, d):
        if d == 1:
            return pl.ds(pl.multiple_of(start, A_BLOCK), n)
        return pl.ds(start, n, stride=d)

    for pi, d in enumerate(A_DILATIONS):
        n_blk = span // A_BLOCK

        def body(blk, carry, d=d, pi=pi):
            if d == 1:
                sp, r = blk, 0
            elif d * A_BLOCK == span:
                sp, r = 0, blk
            else:
                sp, r = blk // d, blk % d
            q0 = sp * (A_BLOCK * d) + r
            qb = q_ref[rows(q0, A_BLOCK, d), :]
            k0 = span + q0 - A_BLOCK * d
            kb = kbuf[rows(k0, 2 * A_BLOCK, d), :].astype(BF16)
            vb = vbuf[rows(k0, 2 * A_BLOCK, d), :].astype(BF16)
            mask = bias[jnp.where(jnp.logical_or(i > 0, sp > 0), 0, 1)]
            q2 = jnp.concatenate([jnp.where(lo, qb, 0.0), jnp.where(lo, 0.0, qb)], axis=0).astype(BF16)
            s = lax.dot_general(q2, kb, (((1,), (1,)), ((), ())), preferred_element_type=F32)
            s = s + jnp.concatenate([mask, mask], axis=0)
            m_h = jnp.max(s, axis=-1, keepdims=True)
            p = jnp.exp2(s - m_h)
            l_h = jnp.sum(p, axis=-1, keepdims=True)
            o_h = jnp.dot(p.astype(BF16), vb, preferred_element_type=F32)
            qrows = rows(q0, A_BLOCK, d)
            acc[pi, qrows, :] = jnp.where(lo, o_h[:A_BLOCK], o_h[A_BLOCK:])
            mst[pi, qrows, :] = jnp.where(lo, m_h[:A_BLOCK], m_h[A_BLOCK:])
            lst[pi, qrows, :] = jnp.where(lo, l_h[:A_BLOCK], l_h[A_BLOCK:])
            return carry

        lax.fori_loop(0, n_blk, body, 0, unroll=unroll)

    n_pat = len(A_DILATIONS)
    cr = 256
    for c in range(span // cr):
        rs = slice(c * cr, (c + 1) * cr)
        m_p = [mst[pi, rs, :] for pi in range(n_pat)]
        m = functools.reduce(jnp.maximum, m_p)
        w = [jnp.exp2(mp - m) for mp in m_p]
        num = sum(w[pi] * acc[pi, rs, :] for pi in range(n_pat))
        den = sum(w[pi] * lst[pi, rs, :] for pi in range(n_pat))
        o_ref[rs, :] = (num / den).astype(o_ref.dtype)


def _attention(qa, ka, va, batch, seq, unroll=8):
    t = qa.shape[0]
    nt = seq // A_SPAN
    n_slab = A_WIDTH // LANES
    n_pat = len(A_DILATIONS)
    blk = pl.BlockSpec((A_SPAN, LANES), lambda b, s, i: (b * nt + i, s))
    return pl.pallas_call(
        functools.partial(_attn_kernel, unroll=unroll),
        grid=(batch, n_slab, nt),
        in_specs=[blk, blk, blk],
        out_specs=blk,
        out_shape=jax.ShapeDtypeStruct((t, A_WIDTH), BF16),
        scratch_shapes=[pltpu.VMEM((2 * A_SPAN, LANES), F32), pltpu.VMEM((2 * A_SPAN, LANES), F32),
                        pltpu.VMEM((n_pat, A_SPAN, LANES), F32), pltpu.VMEM((n_pat, A_SPAN, LANES), F32),
                        pltpu.VMEM((n_pat, A_SPAN, LANES), F32), pltpu.VMEM((2, A_BLOCK, 2 * A_BLOCK), F32)],
        compiler_params=_cparams("arbitrary", "arbitrary", "arbitrary"),
        name="dilated_attention",
    )(qa, ka, va)


def _retention_tables(chunk):
    log_gamma = jnp.log1p(-jnp.exp2(-5.0 - jnp.arange(R_HEADS, dtype=F32)))
    idx = jnp.arange(chunk, dtype=F32)
    diff = idx[:, None] - idx[None, :]
    decay = jnp.where(diff >= 0, jnp.exp(log_gamma[:, None, None] * jnp.maximum(diff, 0.0)), 0.0)
    zeta = jnp.exp(log_gamma[:, None] * (chunk - 1.0 - idx))
    xi = jnp.exp(log_gamma[:, None] * (idx + 1.0))
    cdec = jnp.exp(log_gamma * chunk)

    def slab(tab):
        tab = tab.reshape(R_HEADS // 2, 2, chunk)
        return jnp.repeat(tab.transpose(0, 2, 1), R_QK_DIM, axis=2)

    cdec_t = jnp.broadcast_to(cdec[:, None, None], (R_HEADS, 1, LANES))
    return decay.astype(F32), slab(zeta).astype(F32), slab(xi).astype(F32), cdec_t.astype(F32)


def _retention_kernel(q_ref, k_ref, v_ref, g_ref, gn_ref, dec_ref, zeta_ref, xi_ref, cdec_ref, o_ref, state,
                      *, chunk):
    @pl.when(pl.program_id(1) == 0)
    def _():
        state[...] = jnp.zeros(state.shape, F32)

    rows_total = q_ref.shape[0]
    lane = lax.broadcasted_iota(jnp.int32, (chunk, LANES), 1)
    lo = lane < R_QK_DIM
    for c in range(rows_total // chunk):
        rs = slice(c * chunk, (c + 1) * chunk)
        for s in range(R_HEADS // 2):
            qs = q_ref[rs, s * LANES:(s + 1) * LANES]
            ks = k_ref[rs, s * LANES:(s + 1) * LANES]
            kz = (ks.astype(F32) * zeta_ref[s]).astype(BF16)
            for hh in range(2):
                h = 2 * s + hh
                mask = lo if hh == 0 else jnp.logical_not(lo)
                qm = jnp.where(mask, qs, jnp.zeros_like(qs))
                vh = v_ref[rs, h * R_V_DIM:(h + 1) * R_V_DIM]
                sc = lax.dot_general(qm, ks, (((1,), (1,)), ((), ())), preferred_element_type=F32)
                sc = (sc * dec_ref[h]).astype(BF16)
                y = jnp.dot(sc, vh, preferred_element_type=F32)
                qx = (qm.astype(F32) * xi_ref[s]).astype(BF16)
                st = state[h]
                y = y + jnp.dot(qx, st.astype(BF16), preferred_element_type=F32)
                kv = lax.dot_general(kz, vh, (((0,), (0,)), ((), ())), preferred_element_type=F32)
                state[h] = cdec_ref[h] * st + kv
                mu = jnp.mean(y, axis=-1, keepdims=True)
                yc = y - mu
                var = jnp.mean(yc * yc, axis=-1, keepdims=True)
                yn = yc * lax.rsqrt(var + 1e-5) * gn_ref[:, h * R_V_DIM:(h + 1) * R_V_DIM]
                g = g_ref[rs, h * R_V_DIM:(h + 1) * R_V_DIM].astype(F32)
                o_ref[rs, h * R_V_DIM:(h + 1) * R_V_DIM] = (g * _sigmoid(g) * yn).astype(o_ref.dtype)


def _retention(qr, kr, vr, gr, gn_g, batch, seq, tr, chunk):
    t = qr.shape[0]
    nt = seq // tr
    decay, zeta, xi, cdec = _retention_tables(chunk)
    row = lambda b, i: (b * nt + i, 0)
    fix2 = lambda b, i: (0, 0)
    fix3 = lambda b, i: (0, 0, 0)
    rqk = R_HEADS * R_QK_DIM
    return pl.pallas_call(
        functools.partial(_retention_kernel, chunk=chunk),
        grid=(batch, nt),
        in_specs=[pl.BlockSpec((tr, rqk), row), pl.BlockSpec((tr, rqk), row),
                  pl.BlockSpec((tr, R_WIDTH), row), pl.BlockSpec((tr, R_WIDTH), row),
                  pl.BlockSpec((1, R_WIDTH), fix2),
                  pl.BlockSpec((R_HEADS, chunk, chunk), fix3),
                  pl.BlockSpec((R_HEADS // 2, chunk, LANES), fix3),
                  pl.BlockSpec((R_HEADS // 2, chunk, LANES), fix3),
                  pl.BlockSpec((R_HEADS, 1, LANES), fix3)],
        out_specs=pl.BlockSpec((tr, R_WIDTH), row),
        out_shape=jax.ShapeDtypeStruct((t, R_WIDTH), BF16),
        scratch_shapes=[pltpu.VMEM((R_HEADS, LANES, R_V_DIM), F32)],
        compiler_params=_cparams("arbitrary", "arbitrary"),
        name="retention",
    )(qr, kr, vr, gr, gn_g.reshape(1, R_WIDTH), decay, zeta, xi, cdec)


def _ssm_matrices(a_re, a_im, b_re, b_im, c_re, c_im, log_dt):
    ell = S_CHUNK
    g_n, p_n, c_n = S_GROUPS, S_STATE, S_GROUP
    dt = jnp.exp(log_dt.astype(F32))[:, None]
    lam_re, lam_im = a_re.astype(F32), a_im.astype(F32)
    mag = jnp.exp(lam_re * dt)
    abar_re = mag * jnp.cos(lam_im * dt)
    abar_im = mag * jnp.sin(lam_im * dt)
    den = lam_re * lam_re + lam_im * lam_im
    nr, ni = abar_re - 1.0, abar_im
    f_re = ((nr * lam_re + ni * lam_im) / den)[..., None]
    f_im = ((ni * lam_re - nr * lam_im) / den)[..., None]
    br, bi = b_re.astype(F32), b_im.astype(F32)
    bb_re = f_re * br - f_im * bi
    bb_im = f_re * bi + f_im * br
    j = jnp.arange(ell + 1, dtype=F32)[:, None, None]
    pw_mag = jnp.exp(j * (lam_re * dt)[None])
    pw_ang = j * (lam_im * dt)[None]
    pw_re = pw_mag * jnp.cos(pw_ang)
    pw_im = pw_mag * jnp.sin(pw_ang)
    cr, ci = c_re.astype(F32), c_im.astype(F32)
    hi = lax.Precision.HIGHEST
    w_re = cr[None] * pw_re[:, :, None, :] - ci[None] * pw_im[:, :, None, :]
    w_im = cr[None] * pw_im[:, :, None, :] + ci[None] * pw_re[:, :, None, :]
    kern = (jnp.einsum('jgcp,gpd->jgcd', w_re[:ell], bb_re, precision=hi)
            - jnp.einsum('jgcp,gpd->jgcd', w_im[:ell], bb_im, precision=hi))
    lag_rows = kern.transpose(1, 3, 2, 0).reshape(g_n, c_n, c_n * ell)
    toep = pl.pallas_call(
        _toeplitz_kernel,
        grid=(g_n,),
        in_specs=[pl.BlockSpec((1, c_n, c_n * ell), lambda g: (g, 0, 0))],
        out_specs=pl.BlockSpec((1, c_n * ell, c_n * ell), lambda g: (g, 0, 0)),
        out_shape=jax.ShapeDtypeStruct((g_n, c_n * ell, c_n * ell), BF16),
        compiler_params=_cparams("arbitrary"),
        name="ssm_toeplitz",
    )(lag_rows)
    e_re = pw_re[ell - 1 - np.arange(ell)].transpose(1, 0, 2)[:, None]
    e_im = pw_im[ell - 1 - np.arange(ell)].transpose(1, 0, 2)[:, None]
    bbt_re = bb_re.transpose(0, 2, 1)[:, :, None, :]
    bbt_im = bb_im.transpose(0, 2, 1)[:, :, None, :]
    bz_re = (e_re * bbt_re - e_im * bbt_im).reshape(g_n, c_n * ell, p_n)
    bz_im = (e_re * bbt_im + e_im * bbt_re).reshape(g_n, c_n * ell, p_n)
    zeros = jnp.zeros_like(bz_re[0::2])
    top = jnp.concatenate([bz_re[0::2], zeros, bz_im[0::2], zeros], axis=-1)
    bot = jnp.concatenate([zeros, bz_re[1::2], zeros, bz_im[1::2]], axis=-1)
    bz = jnp.concatenate([top, bot], axis=1)
    cz_re = w_re[1:].transpose(1, 3, 2, 0).reshape(g_n, p_n, c_n * ell)
    cz_im = -w_im[1:].transpose(1, 3, 2, 0).reshape(g_n, p_n, c_n * ell)
    zc = jnp.zeros_like(cz_re[0::2])
    cz = jnp.concatenate([
        jnp.concatenate([cz_re[0::2], zc], axis=-1),
        jnp.concatenate([zc, cz_re[1::2]], axis=-1),
        jnp.concatenate([cz_im[0::2], zc], axis=-1),
        jnp.concatenate([zc, cz_im[1::2]], axis=-1)], axis=1)
    al_re = pw_re[ell].reshape(S_PAIRS, 2 * p_n)
    al_im = pw_im[ell].reshape(S_PAIRS, 2 * p_n)
    a_l = jnp.stack([al_re, al_im], axis=0)
    return toep, bz.astype(BF16), cz.astype(BF16), a_l.astype(F32)


def _toeplitz_kernel(k_ref, m_ref):
    ell = S_CHUNK
    width = k_ref.shape[2]
    s_i = lax.broadcasted_iota(jnp.int32, (ell, width), 0)
    t_i = lax.broadcasted_iota(jnp.int32, (ell, width), 1) % ell
    for c in range(k_ref.shape[1]):
        row = jnp.broadcast_to(k_ref[0, c:c + 1, :], (ell, width))
        shifted = pltpu.roll(row, 0, 1, stride=1, stride_axis=0)
        m_ref[0, c * ell:(c + 1) * ell, :] = jnp.where(t_i >= s_i, shifted, 0.0).astype(m_ref.dtype)


def _ssm_state_kernel(u_ref, bz_ref, s_ref):
    s_ref[...] = jnp.dot(u_ref[...], bz_ref[0], preferred_element_type=F32)


def _ssm_scan_kernel(s_ref, al_ref, h_ref, *, batch, n_chunks):
    n_blk = 2 * S_PAIRS
    a_re = [al_ref[0, k:k + 1, :] for k in range(S_PAIRS)]
    a_im = [al_ref[1, k:k + 1, :] for k in range(S_PAIRS)]

    def body(n, carry):
        new = []
        for b in range(batch):
            row = b * n_chunks + n
            h = carry[b * n_blk:(b + 1) * n_blk]
            s_row = s_ref[pl.ds(row, 1), :]
            h_ref[pl.ds(row, 1), :] = jnp.concatenate(h, axis=1)
            for k in range(S_PAIRS):
                hr, hi = h[2 * k], h[2 * k + 1]
                sr = s_row[:, (2 * k) * LANES:(2 * k + 1) * LANES]
                si = s_row[:, (2 * k + 1) * LANES:(2 * k + 2) * LANES]
                new.append(a_re[k] * hr - a_im[k] * hi + sr)
                new.append(a_re[k] * hi + a_im[k] * hr + si)
        return tuple(new)

    init = tuple(jnp.zeros((1, LANES), F32) for _ in range(batch * n_blk))
    lax.fori_loop(0, n_chunks, body, init)


def _ssm_out_kernel(u_ref, toep_ref, h_ref, cz_ref, y_ref):
    half = S_CHUNK * S_GROUP
    cross = jnp.dot(h_ref[...].astype(BF16), cz_ref[0], preferred_element_type=F32)
    for g in range(2):
        sl = slice(g * half, (g + 1) * half)
        y = jnp.dot(u_ref[:, sl], toep_ref[g], preferred_element_type=F32) + cross[:, sl]
        y_ref[:, sl] = y.astype(y_ref.dtype)


def _ssm_conv(us, mats, batch, seq):
    toep, bz, cz, a_l = mats
    t = us.shape[0]
    ell = S_CHUNK
    n_chunks = seq // ell
    n_all = batch * n_chunks
    pair_w = 2 * ell * S_GROUP
    u_t = us.astype(BF16).reshape(n_all, ell, S_CHANNELS).transpose(0, 2, 1).reshape(n_all, S_PAIRS * pair_w)
    st_w = 4 * S_STATE
    s_all = pl.pallas_call(
        _ssm_state_kernel,
        grid=(S_PAIRS,),
        in_specs=[pl.BlockSpec((n_all, pair_w), lambda k: (0, k)),
                  pl.BlockSpec((1, pair_w, st_w), lambda k: (k, 0, 0))],
        out_specs=pl.BlockSpec((n_all, st_w), lambda k: (0, k)),
        out_shape=jax.ShapeDtypeStruct((n_all, S_PAIRS * st_w), F32),
        compiler_params=_cparams("arbitrary"),
        name="ssm_chunk_state",
    )(u_t, bz)
    h_prev = pl.pallas_call(
        functools.partial(_ssm_scan_kernel, batch=batch, n_chunks=n_chunks),
        out_shape=jax.ShapeDtypeStruct((n_all, S_PAIRS * st_w), F32),
        compiler_params=pltpu.CompilerParams(vmem_limit_bytes=VMEM_LIMIT),
        name="ssm_chunk_scan",
    )(s_all, a_l)
    y_t = pl.pallas_call(
        _ssm_out_kernel,
        grid=(S_PAIRS,),
        in_specs=[pl.BlockSpec((n_all, pair_w), lambda k: (0, k)),
                  pl.BlockSpec((2, pair_w // 2, pair_w // 2), lambda k: (k, 0, 0)),
                  pl.BlockSpec((n_all, st_w), lambda k: (0, k)),
                  pl.BlockSpec((1, st_w, pair_w), lambda k: (k, 0, 0))],
        out_specs=pl.BlockSpec((n_all, pair_w), lambda k: (0, k)),
        out_shape=jax.ShapeDtypeStruct((n_all, S_PAIRS * pair_w), BF16),
        compiler_params=_cparams("arbitrary"),
        name="ssm_chunk_out",
    )(u_t, toep, h_prev, cz)
    return y_t.reshape(n_all, S_CHANNELS, ell).transpose(0, 2, 1).reshape(t, S_CHANNELS)


def _out_proj_kernel(*refs, with_router):
    if with_router:
        (h_ref, ya_ref, yr_ref, ys_ref, us_ref, d_ref, gw_ref, gb_ref, w_ref, n2_ref, rt_ref,
         h1_ref, xn_ref, lg_ref) = refs
    else:
        (h_ref, ya_ref, yr_ref, ys_ref, us_ref, d_ref, gw_ref, gb_ref, w_ref, n2_ref,
         h1_ref, xn_ref) = refs
    y = ys_ref[...].astype(F32) + d_ref[...] * us_ref[...]
    z = 0.5 * y * (1.0 + jnp.tanh(math.sqrt(2.0 / math.pi) * (y + 0.044715 * (y * y * y))))
    gate = jnp.dot(z.astype(BF16), gw_ref[...], preferred_element_type=F32) + gb_ref[...]
    yc = (z * _sigmoid(gate)).astype(BF16)
    c1 = A_WIDTH
    c2 = A_WIDTH + R_WIDTH
    acc = jnp.dot(ya_ref[...], w_ref[0:c1, :], preferred_element_type=F32)
    acc = acc + jnp.dot(yr_ref[...], w_ref[c1:c2, :], preferred_element_type=F32)
    acc = acc + jnp.dot(yc, w_ref[c2:, :], preferred_element_type=F32)
    h1 = h_ref[...] + acc
    h1_ref[...] = h1
    ms = jnp.mean(h1 * h1, axis=-1, keepdims=True)
    xn = h1 * lax.rsqrt(ms + EPS) * n2_ref[...]
    if not with_router:
        xn_ref[...] = xn.astype(xn_ref.dtype)
    else:
        n_slab = xn.shape[1] // LANES
        for c in range(n_slab):
            xn_ref[pl.ds(c, xn.shape[0], stride=n_slab), :] = xn[:, c * LANES:(c + 1) * LANES]
        x_hi = xn.astype(BF16)
        x_lo = (xn - x_hi.astype(F32)).astype(BF16)
        both = jnp.dot(x_hi, rt_ref[...], preferred_element_type=F32)
        lg_ref[...] = (both[:, :LANES] + both[:, LANES:]
                       + jnp.dot(x_lo, rt_ref[:, :LANES], preferred_element_type=F32))


def _out_proj(h2d, ya, yr, ys, us, d_skip, glu_w, glu_b, w_out, norm2_g, router, tm):
    t, d = h2d.shape
    with_router = router is not None
    row = lambda i: (i, 0)
    fixed = lambda i: (0, 0)
    in_specs = [pl.BlockSpec((tm, d), row), pl.BlockSpec((tm, A_WIDTH), row), pl.BlockSpec((tm, R_WIDTH), row),
                pl.BlockSpec((tm, S_CHANNELS), row), pl.BlockSpec((tm, S_CHANNELS), row),
                pl.BlockSpec((1, S_CHANNELS), fixed), pl.BlockSpec((S_CHANNELS, S_CHANNELS), fixed),
                pl.BlockSpec((1, S_CHANNELS), fixed), pl.BlockSpec(w_out.shape, fixed), pl.BlockSpec((1, d), fixed)]
    args = [h2d, ya, yr, ys, us, d_skip.reshape(1, -1), glu_w, glu_b.reshape(1, -1), w_out, norm2_g.reshape(1, d)]
    out_shapes = [jax.ShapeDtypeStruct((t, d), F32), jax.ShapeDtypeStruct((t, d), BF16)]
    out_specs = [pl.BlockSpec((tm, d), row), pl.BlockSpec((tm, d), row)]
    if with_router:
        n_slab = d // LANES
        out_shapes[1] = jax.ShapeDtypeStruct((t * n_slab, LANES), F32)
        out_specs[1] = pl.BlockSpec((tm * n_slab, LANES), row)
        rt = jnp.pad(router.astype(F32), ((0, 0), (0, LANES - router.shape[1])))
        rt_hi = rt.astype(BF16)
        rt_lo = (rt - rt_hi.astype(F32)).astype(BF16)
        in_specs.append(pl.BlockSpec((d, 2 * LANES), fixed))
        args.append(jnp.concatenate([rt_hi, rt_lo], axis=1))
        out_shapes.append(jax.ShapeDtypeStruct((t, LANES), F32))
        out_specs.append(pl.BlockSpec((tm, LANES), row))
    return pl.pallas_call(
        functools.partial(_out_proj_kernel, with_router=with_router),
        grid=(t // tm,),
        in_specs=in_specs,
        out_specs=out_specs,
        out_shape=out_shapes,
        compiler_params=_cparams("arbitrary"),
        name="out_proj",
    )(*args)


def _swiglu_chunk(x, wg, wu, wd):
    hg = jnp.dot(x, wg, preferred_element_type=F32)
    hu = jnp.dot(x, wu, preferred_element_type=F32)
    a = (hg * _sigmoid(hg) * hu).astype(BF16)
    return jnp.dot(a, wd, preferred_element_type=F32)


def _ffn_kernel(x_ref, h_ref, wg_ref, wu_ref, wd_ref, o_ref):
    @pl.when(pl.program_id(1) == 0)
    def _():
        o_ref[...] = h_ref[...]

    o_ref[...] += _swiglu_chunk(x_ref[...], wg_ref[...], wu_ref[...], wd_ref[...])


def _ffn(xn, h1, wg, wu, wd, tm, fc):
    t, d = xn.shape
    f = wg.shape[1]
    return pl.pallas_call(
        _ffn_kernel,
        grid=(t // tm, f // fc),
        in_specs=[pl.BlockSpec((tm, d), lambda i, j: (i, 0)), pl.BlockSpec((tm, d), lambda i, j: (i, 0)),
                  pl.BlockSpec((d, fc), lambda i, j: (0, j)), pl.BlockSpec((d, fc), lambda i, j: (0, j)),
                  pl.BlockSpec((fc, d), lambda i, j: (j, 0))],
        out_specs=pl.BlockSpec((tm, d), lambda i, j: (i, 0)),
        out_shape=jax.ShapeDtypeStruct((t, d), F32),
        compiler_params=_cparams("arbitrary", "arbitrary"),
        name="swiglu_ffn",
    )(xn, h1, wg, wu, wd)


def _route_kernel(lg_ref, tri_ref, gate_ref, idx_ref, cnt_ref, carry):
    @pl.when(pl.program_id(0) == 0)
    def _():
        carry[...] = jnp.zeros(carry.shape, F32)

    lg = lg_ref[...]
    lane = lax.broadcasted_iota(jnp.int32, lg.shape, 1)
    valid = lane < N_EXPERTS
    mx = jnp.max(jnp.where(valid, lg, NEG_BIG), axis=-1, keepdims=True)
    ex = jnp.where(valid, jnp.exp(lg - mx), 0.0)
    probs = ex / jnp.sum(ex, axis=-1, keepdims=True)
    p1 = jnp.max(probs, axis=-1, keepdims=True)
    e1 = jnp.min(jnp.where(valid & (probs == p1), lane, LANES), axis=-1, keepdims=True)
    rest = jnp.where(valid & (lane != e1), probs, -1.0)
    p2 = jnp.max(rest, axis=-1, keepdims=True)
    e2 = jnp.min(jnp.where(rest == p2, lane, LANES), axis=-1, keepdims=True)
    den = p1 + p2
    oh1 = lane == e1
    oh2 = lane == e2
    oh = jnp.where(oh1 | oh2, 1.0, 0.0)
    cum = jnp.dot(tri_ref[...], oh.astype(BF16), preferred_element_type=F32)
    excl = cum - oh + carry[...]
    r1 = jnp.sum(jnp.where(oh1, excl, 0.0), axis=-1, keepdims=True)
    r2 = jnp.sum(jnp.where(oh2, excl, 0.0), axis=-1, keepdims=True)
    tot = carry[...] + cum[cum.shape[0] - 1:cum.shape[0], :]
    carry[...] = tot
    cnt_ref[...] = tot.astype(jnp.int32)
    gate_ref[...] = jnp.where(lane == 0, p1 / den, jnp.where(lane == 1, p2 / den, 0.0))
    idx_ref[...] = jnp.where(lane == 0, e1, jnp.where(lane == 1, e2, jnp.where(
        lane == 2, r1.astype(jnp.int32), jnp.where(lane == 3, r2.astype(jnp.int32), 0))))


def _route(logits, tm):
    t = logits.shape[0]
    tri = jnp.asarray(np.tril(np.ones((tm, tm), np.float32)), BF16)
    row = lambda i: (i, 0)
    return pl.pallas_call(
        _route_kernel,
        grid=(t // tm,),
        in_specs=[pl.BlockSpec((tm, LANES), row), pl.BlockSpec((tm, tm), lambda i: (0, 0))],
        out_specs=[pl.BlockSpec((tm, LANES), row), pl.BlockSpec((tm, LANES), row),
                   pl.BlockSpec((1, LANES), lambda i: (0, 0))],
        out_shape=[jax.ShapeDtypeStruct((t, LANES), F32), jax.ShapeDtypeStruct((t, LANES), jnp.int32),
                   jax.ShapeDtypeStruct((1, LANES), jnp.int32)],
        scratch_shapes=[pltpu.VMEM((1, LANES), F32)],
        compiler_params=_cparams("arbitrary"),
        name="moe_route",
    )(logits, tri)


DMA_UNROLL = 8
SLABS = 8


def _slab_rows(row):
    return pl.ds(pl.multiple_of(row * SLABS, SLABS), SLABS)


def _dispatch_kernel(tok_ref, x_hbm, o_ref, buf, sem, *, tm):
    i = pl.program_id(0)
    n = pl.num_programs(0)

    def issue(step, slot_buf):
        def body(jb, c):
            for u in range(DMA_UNROLL):
                j = jb * DMA_UNROLL + u
                tok = tok_ref[step * tm + j]
                pltpu.make_async_copy(x_hbm.at[_slab_rows(tok)], buf.at[slot_buf, _slab_rows(j)],
                                      sem.at[slot_buf]).start(priority=u % 2)
            return c
        lax.fori_loop(0, tm // DMA_UNROLL, body, 0)

    @pl.when(i == 0)
    def _():
        issue(0, 0)

    @pl.when(i + 1 < n)
    def _():
        issue(i + 1, (i + 1) % 2)

    cur = i % 2
    pltpu.make_async_copy(x_hbm.at[pl.ds(0, tm * SLABS)], buf.at[cur], sem.at[cur]).wait()
    o_ref[...] = buf[cur]


def _dispatch(xn, tok_sorted, tm):
    n_rows = tok_sorted.shape[0]
    return pl.pallas_call(
        functools.partial(_dispatch_kernel, tm=tm),
        grid_spec=pltpu.PrefetchScalarGridSpec(
            num_scalar_prefetch=1,
            grid=(n_rows // tm,),
            in_specs=[pl.BlockSpec(memory_space=pl.ANY)],
            out_specs=pl.BlockSpec((tm * SLABS, LANES), lambda i, tok: (i, 0)),
            scratch_shapes=[pltpu.VMEM((2, tm * SLABS, LANES), xn.dtype), pltpu.SemaphoreType.DMA((2,))]),
        out_shape=jax.ShapeDtypeStruct((n_rows * SLABS, LANES), xn.dtype),
        compiler_params=_cparams("arbitrary"),
        name="moe_dispatch",
    )(tok_sorted, xn)


def _moe_kernel(te_ref, nu_ref, x_ref, wg_ref, wu_ref, wd_ref, o_ref, xb, acc):
    i = pl.program_id(0)
    f = pl.program_id(1)
    tm = xb.shape[0]

    @pl.when(i < nu_ref[0])
    def _():
        @pl.when(f == 0)
        def _():
            for c in range(SLABS):
                xb[:, c * LANES:(c + 1) * LANES] = x_ref[pl.ds(c, tm, stride=SLABS), :].astype(BF16)
            acc[...] = jnp.zeros(acc.shape, F32)

        acc[...] += _swiglu_chunk(xb[...], wg_ref[0], wu_ref[0], wd_ref[0])

        @pl.when(f == pl.num_programs(1) - 1)
        def _():
            for c in range(SLABS):
                o_ref[pl.ds(c, tm, stride=SLABS), :] = acc[:, c * LANES:(c + 1) * LANES]

    @pl.when(jnp.logical_and(i >= nu_ref[0], f == 0))
    def _():
        o_ref[...] = jnp.zeros(o_ref.shape, F32)


def _moe_experts(xs, tile_expert, n_used, wg, wu, wd, tm, fc):
    d = wg.shape[1]
    f = wg.shape[2]
    nf = f // fc
    n_tiles = xs.shape[0] // (tm * SLABS)

    def x_map(i, j, te, nu):
        return (jnp.minimum(i, nu[0] - 1), 0)

    def f_eff(i, j, nu):
        return jnp.where(i < nu[0], j, nf - 1)

    return pl.pallas_call(
        _moe_kernel,
        grid_spec=pltpu.PrefetchScalarGridSpec(
            num_scalar_prefetch=2,
            grid=(n_tiles, nf),
            in_specs=[pl.BlockSpec((tm * SLABS, LANES), x_map),
                      pl.BlockSpec((1, d, fc), lambda i, j, te, nu: (te[i], 0, f_eff(i, j, nu))),
                      pl.BlockSpec((1, d, fc), lambda i, j, te, nu: (te[i], 0, f_eff(i, j, nu))),
                      pl.BlockSpec((1, fc, d), lambda i, j, te, nu: (te[i], f_eff(i, j, nu), 0))],
            out_specs=pl.BlockSpec((tm * SLABS, LANES), lambda i, j, te, nu: (i, 0)),
            scratch_shapes=[pltpu.VMEM((tm, d), BF16), pltpu.VMEM((tm, d), F32)]),
        out_shape=jax.ShapeDtypeStruct(xs.shape, F32),
        compiler_params=_cparams("arbitrary", "arbitrary"),
        name="moe_experts",
    )(tile_expert, n_used, xs, wg, wu, wd)


def _combine_kernel(pos_ref, h_ref, gate_ref, ys_hbm, o_ref, buf, sem, *, tc):
    i = pl.program_id(0)
    n = pl.num_programs(0)

    def issue(step, slot_buf):
        def body(jb, c):
            for u in range(DMA_UNROLL):
                j = jb * DMA_UNROLL + u
                for k in range(2):
                    src = pos_ref[2 * (step * tc + j) + k]
                    pltpu.make_async_copy(ys_hbm.at[_slab_rows(src)], buf.at[slot_buf, k, _slab_rows(j)],
                                          sem.at[slot_buf]).start(priority=k)
            return c
        lax.fori_loop(0, tc // DMA_UNROLL, body, 0)

    @pl.when(i == 0)
    def _():
        issue(0, 0)

    @pl.when(i + 1 < n)
    def _():
        issue(i + 1, (i + 1) % 2)

    cur = i % 2
    for k in range(2):
        pltpu.make_async_copy(ys_hbm.at[pl.ds(0, tc * SLABS)], buf.at[cur, k], sem.at[cur]).wait()
    g = gate_ref[...]
    for c in range(SLABS):
        cols = slice(c * LANES, (c + 1) * LANES)
        o_ref[:, cols] = (h_ref[:, cols] + g[:, 0:1] * buf[cur, 0, pl.ds(c, tc, stride=SLABS), :]
                          + g[:, 1:2] * buf[cur, 1, pl.ds(c, tc, stride=SLABS), :])


def _combine(h1, gates, ys, pos_flat, tc):
    t, d = h1.shape
    row = lambda i, p: (i, 0)
    return pl.pallas_call(
        functools.partial(_combine_kernel, tc=tc),
        grid_spec=pltpu.PrefetchScalarGridSpec(
            num_scalar_prefetch=1,
            grid=(t // tc,),
            in_specs=[pl.BlockSpec((tc, d), row), pl.BlockSpec((tc, LANES), row),
                      pl.BlockSpec(memory_space=pl.ANY)],
            out_specs=pl.BlockSpec((tc, d), row),
            scratch_shapes=[pltpu.VMEM((2, 2, tc * SLABS, LANES), F32), pltpu.SemaphoreType.DMA((2,))]),
        out_shape=jax.ShapeDtypeStruct((t, d), F32),
        compiler_params=_cparams("arbitrary"),
        name="moe_combine",
    )(pos_flat, h1, gates, ys)


def _moe(xn, h1, logits, wg, wu, wd, tm, fc):
    t = h1.shape[0]
    gates, idx, counts = _route(logits, 512)
    counts = counts[0, :N_EXPERTS]
    tiles_per = (counts + tm - 1) // tm
    tile_end = jnp.cumsum(tiles_per)
    group_start = (tile_end - tiles_per) * tm
    pos = group_start[idx[:, 0:2]] + idx[:, 2:4]
    pos_flat = pos.reshape(-1).astype(jnp.int32)
    n_tiles = (2 * t) // tm + N_EXPERTS
    n_used = tile_end[-1].astype(jnp.int32).reshape(1)
    tile_ids = jnp.minimum(jnp.arange(n_tiles, dtype=jnp.int32), n_used[0] - 1)
    tile_expert = jnp.sum(tile_ids[:, None] >= tile_end[None, :], axis=1).astype(jnp.int32)
    n_holes = n_tiles * tm - 2 * t
    hole_cnt = tiles_per * tm - counts
    hole_end = jnp.cumsum(hole_cnt)
    j = jnp.arange(n_holes, dtype=jnp.int32)
    he = jnp.sum(j[:, None] >= hole_end[None, :], axis=1)
    onehot = he[:, None] == jnp.arange(N_EXPERTS, dtype=jnp.int32)[None, :]
    in_group = jnp.sum(jnp.where(onehot, (group_start + counts - (hole_end - hole_cnt))[None, :], 0), axis=1) + j
    in_tail = n_used[0] * tm + (j - hole_end[-1])
    hole_pos = jnp.where(he < N_EXPERTS, in_group, in_tail).astype(jnp.int32)
    keys = jnp.concatenate([pos_flat, hole_pos]).astype(jnp.uint32)
    vals = jnp.concatenate([jnp.arange(2 * t, dtype=jnp.uint32) // 2, jnp.zeros((n_holes,), jnp.uint32)])
    tok_bits = max(1, (t - 1).bit_length())
    assert n_tiles * tm <= 1 << (32 - tok_bits), "row index and token index must pack into 32 bits"
    packed = lax.sort(keys * jnp.uint32(1 << tok_bits) + vals)
    tok_sorted = (packed & jnp.uint32((1 << tok_bits) - 1)).astype(jnp.int32)
    xs = _dispatch(xn, tok_sorted, tm)
    ys = _moe_experts(xs, tile_expert, n_used, wg, wu, wd, tm, fc)
    return _combine(h1, gates, ys, pos_flat, 256)


def _trunk(x, norm1_g, w_in, q_norm_g, k_norm_g, ret_gn_g, ssm_a_re, ssm_a_im, ssm_b_re, ssm_b_im,
           ssm_c_re, ssm_c_im, ssm_d, ssm_log_dt, ssm_glu_w, ssm_glu_b, w_out, norm2_g,
           ffn_w_gate, ffn_w_up, ffn_w_down, moe_router, moe_w_gate, moe_w_up, moe_w_down,
           *, tm=512, tr=1024, r_chunk=256, ffn_fc=1408, moe_tm=1024, moe_fc=896):
    batch, seq, d = x.shape
    depth = norm1_g.shape[0]
    t = batch * seq
    tabs = _rope_tables(seq, ROPE_DIM, ROPE_THETA) + _rope_tables(seq, R_QK_DIM, R_ROPE_THETA)
    h = x.reshape(t, d).astype(F32)
    for layer in range(depth):
        qa, ka, va, qr, kr, vr, gr, us = _in_proj(
            h, seq, norm1_g[layer], w_in[layer].astype(BF16), q_norm_g[layer], k_norm_g[layer], tabs, tm)
        ya = _attention(qa, ka, va, batch, seq)
        yr = _retention(qr, kr, vr, gr, ret_gn_g[layer], batch, seq, tr, r_chunk)
        mats = _ssm_matrices(ssm_a_re[layer], ssm_a_im[layer], ssm_b_re[layer], ssm_b_im[layer],
                             ssm_c_re[layer], ssm_c_im[layer], ssm_log_dt[layer])
        ys = _ssm_conv(us, mats, batch, seq)
        i = layer // 2
        router = moe_router[i] if layer % 2 == 1 else None
        outs = _out_proj(h, ya, yr, ys, us, ssm_d[layer], ssm_glu_w[layer].astype(BF16), ssm_glu_b[layer],
                         w_out[layer].astype(BF16), norm2_g[layer], router, tm)
        if layer % 2 == 0:
            h1, xn = outs
            h = _ffn(xn, h1, ffn_w_gate[i].astype(BF16), ffn_w_up[i].astype(BF16), ffn_w_down[i].astype(BF16),
                     2 * tm, ffn_fc)
        else:
            h1, xn, logits = outs
            h = _moe(xn, h1, logits, moe_w_gate[i].astype(BF16), moe_w_up[i].astype(BF16),
                     moe_w_down[i].astype(BF16), moe_tm, moe_fc)
    return h.reshape(batch, seq, d).astype(x.dtype)


def kernel(x, norm1_g, w_in, q_norm_g, k_norm_g, ret_gn_g, ssm_a_re, ssm_a_im, ssm_b_re, ssm_b_im, ssm_c_re,
           ssm_c_im, ssm_d, ssm_log_dt, ssm_glu_w, ssm_glu_b, w_out, norm2_g, ffn_w_gate, ffn_w_up, ffn_w_down,
           moe_router, moe_w_gate, moe_w_up, moe_w_down):
    return _trunk(x, norm1_g, w_in, q_norm_g, k_norm_g, ret_gn_g, ssm_a_re, ssm_a_im, ssm_b_re, ssm_b_im,
                  ssm_c_re, ssm_c_im, ssm_d, ssm_log_dt, ssm_glu_w, ssm_glu_b, w_out, norm2_g,
                  ffn_w_gate, ffn_w_up, ffn_w_down, moe_router, moe_w_gate, moe_w_up, moe_w_down)
```

```python
import functools
import math

import jax
import jax.numpy as jnp
import numpy as np
from jax import lax
from jax.experimental import pallas as pl
from jax.experimental.pallas import tpu as pltpu

F32 = jnp.float32
BF16 = jnp.bfloat16

LANES = 128
EPS = 1e-6
HEAD_DIM = 64
A_HEADS = 4
A_BLOCK = 128
A_DILATIONS = (1, 4, 16)
A_SPAN = A_BLOCK * max(A_DILATIONS)
ROPE_THETA = 500000.0
ROPE_DIM = HEAD_DIM // 4
R_HEADS = 4
R_QK_DIM = 64
R_V_DIM = 128
R_ROPE_THETA = 10000.0
S_CHANNELS = 256
S_GROUP = 16
S_GROUPS = S_CHANNELS // S_GROUP
S_STATE = 64
S_CHUNK = 32
S_PAIRS = S_GROUPS // 2
A_WIDTH = A_HEADS * HEAD_DIM
R_WIDTH = R_HEADS * R_V_DIM
N_EXPERTS = 8
NEG_BIG = -1e30

VMEM_LIMIT = 56 * 1024 * 1024


def _cparams(*sem):
    return pltpu.CompilerParams(dimension_semantics=sem, vmem_limit_bytes=VMEM_LIMIT)


def _sigmoid(x):
    return 1.0 / (1.0 + jnp.exp(-x))


def _rope_tables(seq, rot_dim, theta):
    half = rot_dim // 2
    inv = jnp.power(theta, -jnp.arange(half, dtype=F32) * 2.0 / rot_dim)
    ang = jnp.arange(seq, dtype=jnp.int32).astype(F32)[:, None] * inv[None, :]
    d = np.arange(LANES) % HEAD_DIM
    idx = jnp.asarray(d % half)
    cos = jnp.cos(ang)[:, idx]
    sin = jnp.sin(ang)[:, idx]
    in_rot = jnp.asarray(d < rot_dim)[None, :]
    first = jnp.asarray(d < half)[None, :]
    cos_t = jnp.where(in_rot, cos, 1.0)
    sin_t = jnp.where(in_rot, jnp.where(first, -sin, sin), 0.0)
    return cos_t.astype(F32), sin_t.astype(F32)


def _rope_slab(x, cos_t, sin_t, half):
    lane = lax.broadcasted_iota(jnp.int32, x.shape, 1)
    fwd = pltpu.roll(x, LANES - half, 1)
    bwd = pltpu.roll(x, half, 1)
    partner = jnp.where((lane % (2 * half)) < half, fwd, bwd)
    return x * cos_t + partner * sin_t


def _head_rms_slab(x, g):
    lane = lax.broadcasted_iota(jnp.int32, x.shape, 1)
    lo = lane < HEAD_DIM
    x2 = x * x
    s0 = jnp.sum(jnp.where(lo, x2, 0.0), axis=-1, keepdims=True)
    s1 = jnp.sum(jnp.where(lo, 0.0, x2), axis=-1, keepdims=True)
    ms = jnp.where(lo, s0, s1) * (1.0 / HEAD_DIM)
    return x * lax.rsqrt(ms + EPS) * g


def _in_proj_kernel(x_ref, g_ref, w_ref, qg_ref, kg_ref, ca_ref, sa_ref, cr_ref, sr_ref,
                    qa_ref, ka_ref, va_ref, qr_ref, kr_ref, vr_ref, gr_ref, us_ref):
    x = x_ref[...]
    ms = jnp.mean(x * x, axis=-1, keepdims=True)
    xn = (x * lax.rsqrt(ms + EPS) * g_ref[...]).astype(BF16)

    def proj(c0, n):
        return jnp.dot(xn, w_ref[:, c0:c0 + n], preferred_element_type=F32)

    ca, sa, cr, sr = ca_ref[...], sa_ref[...], cr_ref[...], sr_ref[...]
    qa = proj(0, A_WIDTH)
    ka = proj(A_WIDTH, A_WIDTH)
    for s in range(A_WIDTH // LANES):
        sl = slice(s * LANES, (s + 1) * LANES)
        qn = _rope_slab(_head_rms_slab(qa[:, sl], qg_ref[...]), ca, sa, ROPE_DIM // 2)
        qa_ref[:, sl] = qn * (math.log2(math.e) * HEAD_DIM ** -0.5)
        ka_ref[:, sl] = _rope_slab(_head_rms_slab(ka[:, sl], kg_ref[...]), ca, sa, ROPE_DIM // 2)
    va_ref[...] = proj(2 * A_WIDTH, A_WIDTH)
    c0 = 3 * A_WIDTH
    rqk = R_HEADS * R_QK_DIM
    qr = proj(c0, rqk)
    kr = proj(c0 + rqk, rqk)
    for s in range(rqk // LANES):
        sl = slice(s * LANES, (s + 1) * LANES)
        qr_ref[:, sl] = _rope_slab(qr[:, sl], cr, sr, R_QK_DIM // 2).astype(BF16)
        kr_ref[:, sl] = (_rope_slab(kr[:, sl], cr, sr, R_QK_DIM // 2) * (R_QK_DIM ** -0.5)).astype(BF16)
    c0 += 2 * rqk
    vr_ref[...] = proj(c0, R_WIDTH).astype(BF16)
    gr_ref[...] = proj(c0 + R_WIDTH, R_WIDTH).astype(BF16)
    us_ref[...] = proj(c0 + 2 * R_WIDTH, S_CHANNELS)


def _in_proj(h2d, seq, norm_g, w_in, q_g, k_g, tabs, tm):
    t, d = h2d.shape
    n_cols = w_in.shape[1]
    nt_seq = seq // tm
    row = lambda i: (i, 0)
    fixed = lambda i: (0, 0)
    tab = lambda i: (i % nt_seq, 0)
    out_shapes = (
        jax.ShapeDtypeStruct((t, A_WIDTH), F32), jax.ShapeDtypeStruct((t, A_WIDTH), F32),
        jax.ShapeDtypeStruct((t, A_WIDTH), F32),
        jax.ShapeDtypeStruct((t, R_HEADS * R_QK_DIM), BF16), jax.ShapeDtypeStruct((t, R_HEADS * R_QK_DIM), BF16),
        jax.ShapeDtypeStruct((t, R_WIDTH), BF16), jax.ShapeDtypeStruct((t, R_WIDTH), BF16),
        jax.ShapeDtypeStruct((t, S_CHANNELS), F32))
    return pl.pallas_call(
        _in_proj_kernel,
        grid=(t // tm,),
        in_specs=[pl.BlockSpec((tm, d), row), pl.BlockSpec((1, d), fixed), pl.BlockSpec((d, n_cols), fixed),
                  pl.BlockSpec((1, LANES), fixed), pl.BlockSpec((1, LANES), fixed)]
                 + [pl.BlockSpec((tm, LANES), tab)] * 4,
        out_specs=[pl.BlockSpec((tm, s.shape[1]), row) for s in out_shapes],
        out_shape=out_shapes,
        compiler_params=_cparams("arbitrary"),
        name="in_proj",
    )(h2d, norm_g.reshape(1, d), w_in, jnp.tile(q_g, 2).reshape(1, LANES), jnp.tile(k_g, 2).reshape(1, LANES), *tabs)


def _attn_kernel(q_ref, k_ref, v_ref, o_ref, kbuf, vbuf, acc, mst, lst, bias, *, unroll):
    i = pl.program_id(2)
    span = A_SPAN

    @pl.when(i == 0)
    def _():
        kbuf[0:span, :] = jnp.zeros((span, LANES), F32)
        vbuf[0:span, :] = jnp.zeros((span, LANES), F32)

    @pl.when(i > 0)
    def _():
        kbuf[0:span, :] = kbuf[span:2 * span, :]
        vbuf[0:span, :] = vbuf[span:2 * span, :]

    kbuf[span:2 * span, :] = k_ref[...]
    vbuf[span:2 * span, :] = v_ref[...]

    qi = lax.broadcasted_iota(jnp.int32, (A_BLOCK, 2 * A_BLOCK), 0)
    kj = lax.broadcasted_iota(jnp.int32, (A_BLOCK, 2 * A_BLOCK), 1)
    bias[0] = jnp.where(kj >= qi, jnp.where(kj <= qi + A_BLOCK, 0.0, NEG_BIG), NEG_BIG)
    bias[1] = jnp.where(kj >= jnp.maximum(qi, A_BLOCK), jnp.where(kj <= qi + A_BLOCK, 0.0, NEG_BIG), NEG_BIG)
    lane = lax.broadcasted_iota(jnp.int32, (A_BLOCK, LANES), 1)
    lo = lane < HEAD_DIM

    def rows(start, n, d):
        if d == 1:
            return pl.ds(pl.multiple_of(start, A_BLOCK), n)
        return pl.ds(start, n, stride=d)

    for pi, d in enumerate(A_DILATIONS):
        n_blk = span // A_BLOCK

        def body(blk, carry, d=d, pi=pi):
            if d == 1:
                sp, r = blk, 0
            elif d * A_BLOCK == span:
                sp, r = 0, blk
            else:
                sp, r = blk // d, blk % d
            q0 = sp * (A_BLOCK * d) + r
            qb = q_ref[rows(q0, A_BLOCK, d), :]
            k0 = span + q0 - A_BLOCK * d
            kb = kbuf[rows(k0, 2 * A_BLOCK, d), :].astype(BF16)
            vb = vbuf[rows(k0, 2 * A_BLOCK, d), :].astype(BF16)
            mask = bias[jnp.where(jnp.logical_or(i > 0, sp > 0), 0, 1)]
            q2 = jnp.concatenate([jnp.where(lo, qb, 0.0), jnp.where(lo, 0.0, qb)], axis=0).astype(BF16)
            s = lax.dot_general(q2, kb, (((1,), (1,)), ((), ())), preferred_element_type=F32)
            s = s + jnp.concatenate([mask, mask], axis=0)
            m_h = jnp.max(s, axis=-1, keepdims=True)
            p = jnp.exp2(s - m_h)
            l_h = jnp.sum(p, axis=-1, keepdims=True)
            o_h = jnp.dot(p.astype(BF16), vb, preferred_element_type=F32)
            qrows = rows(q0, A_BLOCK, d)
            acc[pi, qrows, :] = jnp.where(lo, o_h[:A_BLOCK], o_h[A_BLOCK:])
            mst[pi, qrows, :] = jnp.where(lo, m_h[:A_BLOCK], m_h[A_BLOCK:])
            lst[pi, qrows, :] = jnp.where(lo, l_h[:A_BLOCK], l_h[A_BLOCK:])
            return carry

        lax.fori_loop(0, n_blk, body, 0, unroll=unroll)

    n_pat = len(A_DILATIONS)
    cr = 256
    for c in range(span // cr):
        rs = slice(c * cr, (c + 1) * cr)
        m_p = [mst[pi, rs, :] for pi in range(n_pat)]
        m = functools.reduce(jnp.maximum, m_p)
        w = [jnp.exp2(mp - m) for mp in m_p]
        num = sum(w[pi] * acc[pi, rs, :] for pi in range(n_pat))
        den = sum(w[pi] * lst[pi, rs, :] for pi in range(n_pat))
        o_ref[rs, :] = (num / den).astype(o_ref.dtype)


def _attention(qa, ka, va, batch, seq, unroll=8):
    t = qa.shape[0]
    nt = seq // A_SPAN
    n_slab = A_WIDTH // LANES
    n_pat = len(A_DILATIONS)
    blk = pl.BlockSpec((A_SPAN, LANES), lambda b, s, i: (b * nt + i, s))
    return pl.pallas_call(
        functools.partial(_attn_kernel, unroll=unroll),
        grid=(batch, n_slab, nt),
        in_specs=[blk, blk, blk],
        out_specs=blk,
        out_shape=jax.ShapeDtypeStruct((t, A_WIDTH), BF16),
        scratch_shapes=[pltpu.VMEM((2 * A_SPAN, LANES), F32), pltpu.VMEM((2 * A_SPAN, LANES), F32),
                        pltpu.VMEM((n_pat, A_SPAN, LANES), F32), pltpu.VMEM((n_pat, A_SPAN, LANES), F32),
                        pltpu.VMEM((n_pat, A_SPAN, LANES), F32), pltpu.VMEM((2, A_BLOCK, 2 * A_BLOCK), F32)],
        compiler_params=_cparams("arbitrary", "arbitrary", "arbitrary"),
        name="dilated_attention",
    )(qa, ka, va)


def _retention_tables(chunk):
    log_gamma = jnp.log1p(-jnp.exp2(-5.0 - jnp.arange(R_HEADS, dtype=F32)))
    idx = jnp.arange(chunk, dtype=F32)
    diff = idx[:, None] - idx[None, :]
    decay = jnp.where(diff >= 0, jnp.exp(log_gamma[:, None, None] * jnp.maximum(diff, 0.0)), 0.0)
    zeta = jnp.exp(log_gamma[:, None] * (chunk - 1.0 - idx))
    xi = jnp.exp(log_gamma[:, None] * (idx + 1.0))
    cdec = jnp.exp(log_gamma * chunk)

    def slab(tab):
        tab = tab.reshape(R_HEADS // 2, 2, chunk)
        return jnp.repeat(tab.transpose(0, 2, 1), R_QK_DIM, axis=2)

    cdec_t = jnp.broadcast_to(cdec[:, None, None], (R_HEADS, 1, LANES))
    return decay.astype(F32), slab(zeta).astype(F32), slab(xi).astype(F32), cdec_t.astype(F32)


def _retention_kernel(q_ref, k_ref, v_ref, g_ref, gn_ref, dec_ref, zeta_ref, xi_ref, cdec_ref, o_ref, state,
                      *, chunk):
    @pl.when(pl.program_id(1) == 0)
    def _():
        state[...] = jnp.zeros(state.shape, F32)

    rows_total = q_ref.shape[0]
    lane = lax.broadcasted_iota(jnp.int32, (chunk, LANES), 1)
    lo = lane < R_QK_DIM
    for c in range(rows_total // chunk):
        rs = slice(c * chunk, (c + 1) * chunk)
        for s in range(R_HEADS // 2):
            qs = q_ref[rs, s * LANES:(s + 1) * LANES]
            ks = k_ref[rs, s * LANES:(s + 1) * LANES]
            kz = (ks.astype(F32) * zeta_ref[s]).astype(BF16)
            for hh in range(2):
                h = 2 * s + hh
                mask = lo if hh == 0 else jnp.logical_not(lo)
                qm = jnp.where(mask, qs, jnp.zeros_like(qs))
                vh = v_ref[rs, h * R_V_DIM:(h + 1) * R_V_DIM]
                sc = lax.dot_general(qm, ks, (((1,), (1,)), ((), ())), preferred_element_type=F32)
                sc = (sc * dec_ref[h]).astype(BF16)
                y = jnp.dot(sc, vh, preferred_element_type=F32)
                qx = (qm.astype(F32) * xi_ref[s]).astype(BF16)
                st = state[h]
                y = y + jnp.dot(qx, st.astype(BF16), preferred_element_type=F32)
                kv = lax.dot_general(kz, vh, (((0,), (0,)), ((), ())), preferred_element_type=F32)
                state[h] = cdec_ref[h] * st + kv
                mu = jnp.mean(y, axis=-1, keepdims=True)
                yc = y - mu
                var = jnp.mean(yc * yc, axis=-1, keepdims=True)
                yn = yc * lax.rsqrt(var + 1e-5) * gn_ref[:, h * R_V_DIM:(h + 1) * R_V_DIM]
                g = g_ref[rs, h * R_V_DIM:(h + 1) * R_V_DIM].astype(F32)
                o_ref[rs, h * R_V_DIM:(h + 1) * R_V_DIM] = (g * _sigmoid(g) * yn).astype(o_ref.dtype)


def _retention(qr, kr, vr, gr, gn_g, batch, seq, tr, chunk):
    t = qr.shape[0]
    nt = seq // tr
    decay, zeta, xi, cdec = _retention_tables(chunk)
    row = lambda b, i: (b * nt + i, 0)
    fix2 = lambda b, i: (0, 0)
    fix3 = lambda b, i: (0, 0, 0)
    rqk = R_HEADS * R_QK_DIM
    return pl.pallas_call(
        functools.partial(_retention_kernel, chunk=chunk),
        grid=(batch, nt),
        in_specs=[pl.BlockSpec((tr, rqk), row), pl.BlockSpec((tr, rqk), row),
                  pl.BlockSpec((tr, R_WIDTH), row), pl.BlockSpec((tr, R_WIDTH), row),
                  pl.BlockSpec((1, R_WIDTH), fix2),
                  pl.BlockSpec((R_HEADS, chunk, chunk), fix3),
                  pl.BlockSpec((R_HEADS // 2, chunk, LANES), fix3),
                  pl.BlockSpec((R_HEADS // 2, chunk, LANES), fix3),
                  pl.BlockSpec((R_HEADS, 1, LANES), fix3)],
        out_specs=pl.BlockSpec((tr, R_WIDTH), row),
        out_shape=jax.ShapeDtypeStruct((t, R_WIDTH), BF16),
        scratch_shapes=[pltpu.VMEM((R_HEADS, LANES, R_V_DIM), F32)],
        compiler_params=_cparams("arbitrary", "arbitrary"),
        name="retention",
    )(qr, kr, vr, gr, gn_g.reshape(1, R_WIDTH), decay, zeta, xi, cdec)


def _ssm_matrices(a_re, a_im, b_re, b_im, c_re, c_im, log_dt):
    ell = S_CHUNK
    g_n, p_n, c_n = S_GROUPS, S_STATE, S_GROUP
    dt = jnp.exp(log_dt.astype(F32))[:, None]
    lam_re, lam_im = a_re.astype(F32), a_im.astype(F32)
    mag = jnp.exp(lam_re * dt)
    abar_re = mag * jnp.cos(lam_im * dt)
    abar_im = mag * jnp.sin(lam_im * dt)
    den = lam_re * lam_re + lam_im * lam_im
    nr, ni = abar_re - 1.0, abar_im
    f_re = ((nr * lam_re + ni * lam_im) / den)[..., None]
    f_im = ((ni * lam_re - nr * lam_im) / den)[..., None]
    br, bi = b_re.astype(F32), b_im.astype(F32)
    bb_re = f_re * br - f_im * bi
    bb_im = f_re * bi + f_im * br
    j = jnp.arange(ell + 1, dtype=F32)[:, None, None]
    pw_mag = jnp.exp(j * (lam_re * dt)[None])
    pw_ang = j * (lam_im * dt)[None]
    pw_re = pw_mag * jnp.cos(pw_ang)
    pw_im = pw_mag * jnp.sin(pw_ang)
    cr, ci = c_re.astype(F32), c_im.astype(F32)
    hi = lax.Precision.HIGHEST
    w_re = cr[None] * pw_re[:, :, None, :] - ci[None] * pw_im[:, :, None, :]
    w_im = cr[None] * pw_im[:, :, None, :] + ci[None] * pw_re[:, :, None, :]
    kern = (jnp.einsum('jgcp,gpd->jgcd', w_re[:ell], bb_re, precision=hi)
            - jnp.einsum('jgcp,gpd->jgcd', w_im[:ell], bb_im, precision=hi))
    lag_rows = kern.transpose(1, 3, 2, 0).reshape(g_n, c_n, c_n * ell)
    toep = pl.pallas_call(
        _toeplitz_kernel,
        grid=(g_n,),
        in_specs=[pl.BlockSpec((1, c_n, c_n * ell), lambda g: (g, 0, 0))],
        out_specs=pl.BlockSpec((1, c_n * ell, c_n * ell), lambda g: (g, 0, 0)),
        out_shape=jax.ShapeDtypeStruct((g_n, c_n * ell, c_n * ell), BF16),
        compiler_params=_cparams("arbitrary"),
        name="ssm_toeplitz",
    )(lag_rows)
    e_re = pw_re[ell - 1 - np.arange(ell)].transpose(1, 0, 2)[:, None]
    e_im = pw_im[ell - 1 - np.arange(ell)].transpose(1, 0, 2)[:, None]
    bbt_re = bb_re.transpose(0, 2, 1)[:, :, None, :]
    bbt_im = bb_im.transpose(0, 2, 1)[:, :, None, :]
    bz_re = (e_re * bbt_re - e_im * bbt_im).reshape(g_n, c_n * ell, p_n)
    bz_im = (e_re * bbt_im + e_im * bbt_re).reshape(g_n, c_n * ell, p_n)
    zeros = jnp.zeros_like(bz_re[0::2])
    top = jnp.concatenate([bz_re[0::2], zeros, bz_im[0::2], zeros], axis=-1)
    bot = jnp.concatenate([zeros, bz_re[1::2], zeros, bz_im[1::2]], axis=-1)
    bz = jnp.concatenate([top, bot], axis=1)
    cz_re = w_re[1:].transpose(1, 3, 2, 0).reshape(g_n, p_n, c_n * ell)
    cz_im = -w_im[1:].transpose(1, 3, 2, 0).reshape(g_n, p_n, c_n * ell)
    zc = jnp.zeros_like(cz_re[0::2])
    cz = jnp.concatenate([
        jnp.concatenate([cz_re[0::2], zc], axis=-1),
        jnp.concatenate([zc, cz_re[1::2]], axis=-1),
        jnp.concatenate([cz_im[0::2], zc], axis=-1),
        jnp.concatenate([zc, cz_im[1::2]], axis=-1)], axis=1)
    al_re = pw_re[ell].reshape(S_PAIRS, 2 * p_n)
    al_im = pw_im[ell].reshape(S_PAIRS, 2 * p_n)
    a_l = jnp.stack([al_re, al_im], axis=0)
    return toep, bz.astype(BF16), cz.astype(BF16), a_l.astype(F32)


def _toeplitz_kernel(k_ref, m_ref):
    ell = S_CHUNK
    width = k_ref.shape[2]
    s_i = lax.broadcasted_iota(jnp.int32, (ell, width), 0)
    t_i = lax.broadcasted_iota(jnp.int32, (ell, width), 1) % ell
    for c in range(k_ref.shape[1]):
        row = jnp.broadcast_to(k_ref[0, c:c + 1, :], (ell, width))
        shifted = pltpu.roll(row, 0, 1, stride=1, stride_axis=0)
        m_ref[0, c * ell:(c + 1) * ell, :] = jnp.where(t_i >= s_i, shifted, 0.0).astype(m_ref.dtype)


def _ssm_state_kernel(u_ref, bz_ref, s_ref):
    s_ref[...] = jnp.dot(u_ref[...], bz_ref[0], preferred_element_type=F32)


def _ssm_scan_kernel(s_ref, al_ref, h_ref, *, batch, n_chunks):
    n_blk = 2 * S_PAIRS
    a_re = [al_ref[0, k:k + 1, :] for k in range(S_PAIRS)]
    a_im = [al_ref[1, k:k + 1, :] for k in range(S_PAIRS)]

    def body(n, carry):
        new = []
        for b in range(batch):
            row = b * n_chunks + n
            h = carry[b * n_blk:(b + 1) * n_blk]
            s_row = s_ref[pl.ds(row, 1), :]
            h_ref[pl.ds(row, 1), :] = jnp.concatenate(h, axis=1)
            for k in range(S_PAIRS):
                hr, hi = h[2 * k], h[2 * k + 1]
                sr = s_row[:, (2 * k) * LANES:(2 * k + 1) * LANES]
                si = s_row[:, (2 * k + 1) * LANES:(2 * k + 2) * LANES]
                new.append(a_re[k] * hr - a_im[k] * hi + sr)
                new.append(a_re[k] * hi + a_im[k] * hr + si)
        return tuple(new)

    init = tuple(jnp.zeros((1, LANES), F32) for _ in range(batch * n_blk))
    lax.fori_loop(0, n_chunks, body, init)


def _ssm_out_kernel(u_ref, toep_ref, h_ref, cz_ref, y_ref):
    half = S_CHUNK * S_GROUP
    cross = jnp.dot(h_ref[...].astype(BF16), cz_ref[0], preferred_element_type=F32)
    for g in range(2):
        sl = slice(g * half, (g + 1) * half)
        y = jnp.dot(u_ref[:, sl], toep_ref[g], preferred_element_type=F32) + cross[:, sl]
        y_ref[:, sl] = y.astype(y_ref.dtype)


def _ssm_conv(us, mats, batch, seq):
    toep, bz, cz, a_l = mats
    t = us.shape[0]
    ell = S_CHUNK
    n_chunks = seq // ell
    n_all = batch * n_chunks
    pair_w = 2 * ell * S_GROUP
    u_t = us.astype(BF16).reshape(n_all, ell, S_CHANNELS).transpose(0, 2, 1).reshape(n_all, S_PAIRS * pair_w)
    st_w = 4 * S_STATE
    s_all = pl.pallas_call(
        _ssm_state_kernel,
        grid=(S_PAIRS,),
        in_specs=[pl.BlockSpec((n_all, pair_w), lambda k: (0, k)),
                  pl.BlockSpec((1, pair_w, st_w), lambda k: (k, 0, 0))],
        out_specs=pl.BlockSpec((n_all, st_w), lambda k: (0, k)),
        out_shape=jax.ShapeDtypeStruct((n_all, S_PAIRS * st_w), F32),
        compiler_params=_cparams("arbitrary"),
        name="ssm_chunk_state",
    )(u_t, bz)
    h_prev = pl.pallas_call(
        functools.partial(_ssm_scan_kernel, batch=batch, n_chunks=n_chunks),
        out_shape=jax.ShapeDtypeStruct((n_all, S_PAIRS * st_w), F32),
        compiler_params=pltpu.CompilerParams(vmem_limit_bytes=VMEM_LIMIT),
        name="ssm_chunk_scan",
    )(s_all, a_l)
    y_t = pl.pallas_call(
        _ssm_out_kernel,
        grid=(S_PAIRS,),
        in_specs=[pl.BlockSpec((n_all, pair_w), lambda k: (0, k)),
                  pl.BlockSpec((2, pair_w // 2, pair_w // 2), lambda k: (k, 0, 0)),
                  pl.BlockSpec((n_all, st_w), lambda k: (0, k)),
                  pl.BlockSpec((1, st_w, pair_w), lambda k: (k, 0, 0))],
        out_specs=pl.BlockSpec((n_all, pair_w), lambda k: (0, k)),
        out_shape=jax.ShapeDtypeStruct((n_all, S_PAIRS * pair_w), BF16),
        compiler_params=_cparams("arbitrary"),
        name="ssm_chunk_out",
    )(u_t, toep, h_prev, cz)
    return y_t.reshape(n_all, S_CHANNELS, ell).transpose(0, 2, 1).reshape(t, S_CHANNELS)


def _out_proj_kernel(*refs, with_router):
    if with_router:
        (h_ref, ya_ref, yr_ref, ys_ref, us_ref, d_ref, gw_ref, gb_ref, w_ref, n2_ref, rt_ref,
         h1_ref, xn_ref, lg_ref) = refs
    else:
        (h_ref, ya_ref, yr_ref, ys_ref, us_ref, d_ref, gw_ref, gb_ref, w_ref, n2_ref,
         h1_ref, xn_ref) = refs
    y = ys_ref[...].astype(F32) + d_ref[...] * us_ref[...]
    z = 0.5 * y * (1.0 + jnp.tanh(math.sqrt(2.0 / math.pi) * (y + 0.044715 * (y * y * y))))
    gate = jnp.dot(z.astype(BF16), gw_ref[...], preferred_element_type=F32) + gb_ref[...]
    yc = (z * _sigmoid(gate)).astype(BF16)
    c1 = A_WIDTH
    c2 = A_WIDTH + R_WIDTH
    acc = jnp.dot(ya_ref[...], w_ref[0:c1, :], preferred_element_type=F32)
    acc = acc + jnp.dot(yr_ref[...], w_ref[c1:c2, :], preferred_element_type=F32)
    acc = acc + jnp.dot(yc, w_ref[c2:, :], preferred_element_type=F32)
    h1 = h_ref[...] + acc
    h1_ref[...] = h1
    ms = jnp.mean(h1 * h1, axis=-1, keepdims=True)
    xn = h1 * lax.rsqrt(ms + EPS) * n2_ref[...]
    if not with_router:
        xn_ref[...] = xn.astype(xn_ref.dtype)
    else:
        n_slab = xn.shape[1] // LANES
        for c in range(n_slab):
            xn_ref[pl.ds(c, xn.shape[0], stride=n_slab), :] = xn[:, c * LANES:(c + 1) * LANES]
        x_hi = xn.astype(BF16)
        x_lo = (xn - x_hi.astype(F32)).astype(BF16)
        both = jnp.dot(x_hi, rt_ref[...], preferred_element_type=F32)
        lg_ref[...] = (both[:, :LANES] + both[:, LANES:]
                       + jnp.dot(x_lo, rt_ref[:, :LANES], preferred_element_type=F32))


def _out_proj(h2d, ya, yr, ys, us, d_skip, glu_w, glu_b, w_out, norm2_g, router, tm):
    t, d = h2d.shape
    with_router = router is not None
    row = lambda i: (i, 0)
    fixed = lambda i: (0, 0)
    in_specs = [pl.BlockSpec((tm, d), row), pl.BlockSpec((tm, A_WIDTH), row), pl.BlockSpec((tm, R_WIDTH), row),
                pl.BlockSpec((tm, S_CHANNELS), row), pl.BlockSpec((tm, S_CHANNELS), row),
                pl.BlockSpec((1, S_CHANNELS), fixed), pl.BlockSpec((S_CHANNELS, S_CHANNELS), fixed),
                pl.BlockSpec((1, S_CHANNELS), fixed), pl.BlockSpec(w_out.shape, fixed), pl.BlockSpec((1, d), fixed)]
    args = [h2d, ya, yr, ys, us, d_skip.reshape(1, -1), glu_w, glu_b.reshape(1, -1), w_out, norm2_g.reshape(1, d)]
    out_shapes = [jax.ShapeDtypeStruct((t, d), F32), jax.ShapeDtypeStruct((t, d), BF16)]
    out_specs = [pl.BlockSpec((tm, d), row), pl.BlockSpec((tm, d), row)]
    if with_router:
        n_slab = d // LANES
        out_shapes[1] = jax.ShapeDtypeStruct((t * n_slab, LANES), F32)
        out_specs[1] = pl.BlockSpec((tm * n_slab, LANES), row)
        rt = jnp.pad(router.astype(F32), ((0, 0), (0, LANES - router.shape[1])))
        rt_hi = rt.astype(BF16)
        rt_lo = (rt - rt_hi.astype(F32)).astype(BF16)
        in_specs.append(pl.BlockSpec((d, 2 * LANES), fixed))
        args.append(jnp.concatenate([rt_hi, rt_lo], axis=1))
        out_shapes.append(jax.ShapeDtypeStruct((t, LANES), F32))
        out_specs.append(pl.BlockSpec((tm, LANES), row))
    return pl.pallas_call(
        functools.partial(_out_proj_kernel, with_router=with_router),
        grid=(t // tm,),
        in_specs=in_specs,
        out_specs=out_specs,
        out_shape=out_shapes,
        compiler_params=_cparams("arbitrary"),
        name="out_proj",
    )(*args)


def _swiglu_chunk(x, wg, wu, wd):
    hg = jnp.dot(x, wg, preferred_element_type=F32)
    hu = jnp.dot(x, wu, preferred_element_type=F32)
    a = (hg * _sigmoid(hg) * hu).astype(BF16)
    return jnp.dot(a, wd, preferred_element_type=F32)


def _ffn_kernel(x_ref, h_ref, wg_ref, wu_ref, wd_ref, o_ref):
    @pl.when(pl.program_id(1) == 0)
    def _():
        o_ref[...] = h_ref[...]

    o_ref[...] += _swiglu_chunk(x_ref[...], wg_ref[...], wu_ref[...], wd_ref[...])


def _ffn(xn, h1, wg, wu, wd, tm, fc):
    t, d = xn.shape
    f = wg.shape[1]
    return pl.pallas_call(
        _ffn_kernel,
        grid=(t // tm, f // fc),
        in_specs=[pl.BlockSpec((tm, d), lambda i, j: (i, 0)), pl.BlockSpec((tm, d), lambda i, j: (i, 0)),
                  pl.BlockSpec((d, fc), lambda i, j: (0, j)), pl.BlockSpec((d, fc), lambda i, j: (0, j)),
                  pl.BlockSpec((fc, d), lambda i, j: (j, 0))],
        out_specs=pl.BlockSpec((tm, d), lambda i, j: (i, 0)),
        out_shape=jax.ShapeDtypeStruct((t, d), F32),
        compiler_params=_cparams("arbitrary", "arbitrary"),
        name="swiglu_ffn",
    )(xn, h1, wg, wu, wd)


def _route_kernel(lg_ref, tri_ref, gate_ref, idx_ref, cnt_ref, carry):
    @pl.when(pl.program_id(0) == 0)
    def _():
        carry[...] = jnp.zeros(carry.shape, F32)

    lg = lg_ref[...]
    lane = lax.broadcasted_iota(jnp.int32, lg.shape, 1)
    valid = lane < N_EXPERTS
    mx = jnp.max(jnp.where(valid, lg, NEG_BIG), axis=-1, keepdims=True)
    ex = jnp.where(valid, jnp.exp(lg - mx), 0.0)
    probs = ex / jnp.sum(ex, axis=-1, keepdims=True)
    p1 = jnp.max(probs, axis=-1, keepdims=True)
    e1 = jnp.min(jnp.where(valid & (probs == p1), lane, LANES), axis=-1, keepdims=True)
    rest = jnp.where(valid & (lane != e1), probs, -1.0)
    p2 = jnp.max(rest, axis=-1, keepdims=True)
    e2 = jnp.min(jnp.where(rest == p2, lane, LANES), axis=-1, keepdims=True)
    den = p1 + p2
    oh1 = lane == e1
    oh2 = lane == e2
    oh = jnp.where(oh1 | oh2, 1.0, 0.0)
    cum = jnp.dot(tri_ref[...], oh.astype(BF16), preferred_element_type=F32)
    excl = cum - oh + carry[...]
    r1 = jnp.sum(jnp.where(oh1, excl, 0.0), axis=-1, keepdims=True)
    r2 = jnp.sum(jnp.where(oh2, excl, 0.0), axis=-1, keepdims=True)
    tot = carry[...] + cum[cum.shape[0] - 1:cum.shape[0], :]
    carry[...] = tot
    cnt_ref[...] = tot.astype(jnp.int32)
    gate_ref[...] = jnp.where(lane == 0, p1 / den, jnp.where(lane == 1, p2 / den, 0.0))
    idx_ref[...] = jnp.where(lane == 0, e1, jnp.where(lane == 1, e2, jnp.where(
        lane == 2, r1.astype(jnp.int32), jnp.where(lane == 3, r2.astype(jnp.int32), 0))))


def _route(logits, tm):
    t = logits.shape[0]
    tri = jnp.asarray(np.tril(np.ones((tm, tm), np.float32)), BF16)
    row = lambda i: (i, 0)
    return pl.pallas_call(
        _route_kernel,
        grid=(t // tm,),
        in_specs=[pl.BlockSpec((tm, LANES), row), pl.BlockSpec((tm, tm), lambda i: (0, 0))],
        out_specs=[pl.BlockSpec((tm, LANES), row), pl.BlockSpec((tm, LANES), row),
                   pl.BlockSpec((1, LANES), lambda i: (0, 0))],
        out_shape=[jax.ShapeDtypeStruct((t, LANES), F32), jax.ShapeDtypeStruct((t, LANES), jnp.int32),
                   jax.ShapeDtypeStruct((1, LANES), jnp.int32)],
        scratch_shapes=[pltpu.VMEM((1, LANES), F32)],
        compiler_params=_cparams("arbitrary"),
        name="moe_route",
    )(logits, tri)


DMA_UNROLL = 8
SLABS = 8


def _slab_rows(row):
    return pl.ds(pl.multiple_of(row * SLABS, SLABS), SLABS)


def _dispatch_kernel(tok_ref, x_hbm, o_ref, buf, sem, *, tm):
    i = pl.program_id(0)
    n = pl.num_programs(0)

    def issue(step, slot_buf):
        def body(jb, c):
            for u in range(DMA_UNROLL):
                j = jb * DMA_UNROLL + u
                tok = tok_ref[step * tm + j]
                pltpu.make_async_copy(x_hbm.at[_slab_rows(tok)], buf.at[slot_buf, _slab_rows(j)],
                                      sem.at[slot_buf]).start(priority=u % 2)
            return c
        lax.fori_loop(0, tm // DMA_UNROLL, body, 0)

    @pl.when(i == 0)
    def _():
        issue(0, 0)

    @pl.when(i + 1 < n)
    def _():
        issue(i + 1, (i + 1) % 2)

    cur = i % 2
    pltpu.make_async_copy(x_hbm.at[pl.ds(0, tm * SLABS)], buf.at[cur], sem.at[cur]).wait()
    o_ref[...] = buf[cur]


def _dispatch(xn, tok_sorted, tm):
    n_rows = tok_sorted.shape[0]
    return pl.pallas_call(
        functools.partial(_dispatch_kernel, tm=tm),
        grid_spec=pltpu.PrefetchScalarGridSpec(
            num_scalar_prefetch=1,
            grid=(n_rows // tm,),
            in_specs=[pl.BlockSpec(memory_space=pl.ANY)],
            out_specs=pl.BlockSpec((tm * SLABS, LANES), lambda i, tok: (i, 0)),
            scratch_shapes=[pltpu.VMEM((2, tm * SLABS, LANES), xn.dtype), pltpu.SemaphoreType.DMA((2,))]),
        out_shape=jax.ShapeDtypeStruct((n_rows * SLABS, LANES), xn.dtype),
        compiler_params=_cparams("arbitrary"),
        name="moe_dispatch",
    )(tok_sorted, xn)


def _moe_kernel(te_ref, nu_ref, x_ref, wg_ref, wu_ref, wd_ref, o_ref, xb, acc):
    i = pl.program_id(0)
    f = pl.program_id(1)
    tm = xb.shape[0]

    @pl.when(i < nu_ref[0])
    def _():
        @pl.when(f == 0)
        def _():
            for c in range(SLABS):
                xb[:, c * LANES:(c + 1) * LANES] = x_ref[pl.ds(c, tm, stride=SLABS), :].astype(BF16)
            acc[...] = jnp.zeros(acc.shape, F32)

        acc[...] += _swiglu_chunk(xb[...], wg_ref[0], wu_ref[0], wd_ref[0])

        @pl.when(f == pl.num_programs(1) - 1)
        def _():
            for c in range(SLABS):
                o_ref[pl.ds(c, tm, stride=SLABS), :] = acc[:, c * LANES:(c + 1) * LANES]

    @pl.when(jnp.logical_and(i >= nu_ref[0], f == 0))
    def _():
        o_ref[...] = jnp.zeros(o_ref.shape, F32)


def _moe_experts(xs, tile_expert, n_used, wg, wu, wd, tm, fc):
    d = wg.shape[1]
    f = wg.shape[2]
    nf = f // fc
    n_tiles = xs.shape[0] // (tm * SLABS)

    def x_map(i, j, te, nu):
        return (jnp.minimum(i, nu[0] - 1), 0)

    def f_eff(i, j, nu):
        return jnp.where(i < nu[0], j, nf - 1)

    return pl.pallas_call(
        _moe_kernel,
        grid_spec=pltpu.PrefetchScalarGridSpec(
            num_scalar_prefetch=2,
            grid=(n_tiles, nf),
            in_specs=[pl.BlockSpec((tm * SLABS, LANES), x_map),
                      pl.BlockSpec((1, d, fc), lambda i, j, te, nu: (te[i], 0, f_eff(i, j, nu))),
                      pl.BlockSpec((1, d, fc), lambda i, j, te, nu: (te[i], 0, f_eff(i, j, nu))),
                      pl.BlockSpec((1, fc, d), lambda i, j, te, nu: (te[i], f_eff(i, j, nu), 0))],
            out_specs=pl.BlockSpec((tm * SLABS, LANES), lambda i, j, te, nu: (i, 0)),
            scratch_shapes=[pltpu.VMEM((tm, d), BF16), pltpu.VMEM((tm, d), F32)]),
        out_shape=jax.ShapeDtypeStruct(xs.shape, F32),
        compiler_params=_cparams("arbitrary", "arbitrary"),
        name="moe_experts",
    )(tile_expert, n_used, xs, wg, wu, wd)


def _combine_kernel(pos_ref, h_ref, gate_ref, ys_hbm, o_ref, buf, sem, *, tc):
    i = pl.program_id(0)
    n = pl.num_programs(0)

    def issue(step, slot_buf):
        def body(jb, c):
            for u in range(DMA_UNROLL):
                j = jb * DMA_UNROLL + u
                for k in range(2):
                    src = pos_ref[2 * (step * tc + j) + k]
                    pltpu.make_async_copy(ys_hbm.at[_slab_rows(src)], buf.at[slot_buf, k, _slab_rows(j)],
                                          sem.at[slot_buf]).start(priority=k)
            return c
        lax.fori_loop(0, tc // DMA_UNROLL, body, 0)

    @pl.when(i == 0)
    def _():
        issue(0, 0)

    @pl.when(i + 1 < n)
    def _():
        issue(i + 1, (i + 1) % 2)

    cur = i % 2
    for k in range(2):
        pltpu.make_async_copy(ys_hbm.at[pl.ds(0, tc * SLABS)], buf.at[cur, k], sem.at[cur]).wait()
    g = gate_ref[...]
    for c in range(SLABS):
        cols = slice(c * LANES, (c + 1) * LANES)
        o_ref[:, cols] = (h_ref[:, cols] + g[:, 0:1] * buf[cur, 0, pl.ds(c, tc, stride=SLABS), :]
                          + g[:, 1:2] * buf[cur, 1, pl.ds(c, tc, stride=SLABS), :])


def _combine(h1, gates, ys, pos_flat, tc):
    t, d = h1.shape
    row = lambda i, p: (i, 0)
    return pl.pallas_call(
        functools.partial(_combine_kernel, tc=tc),
        grid_spec=pltpu.PrefetchScalarGridSpec(
            num_scalar_prefetch=1,
            grid=(t // tc,),
            in_specs=[pl.BlockSpec((tc, d), row), pl.BlockSpec((tc, LANES), row),
                      pl.BlockSpec(memory_space=pl.ANY)],
            out_specs=pl.BlockSpec((tc, d), row),
            scratch_shapes=[pltpu.VMEM((2, 2, tc * SLABS, LANES), F32), pltpu.SemaphoreType.DMA((2,))]),
        out_shape=jax.ShapeDtypeStruct((t, d), F32),
        compiler_params=_cparams("arbitrary"),
        name="moe_combine",
    )(pos_flat, h1, gates, ys)


def _moe(xn, h1, logits, wg, wu, wd, tm, fc):
    t = h1.shape[0]
    gates, idx, counts = _route(logits, 512)
    counts = counts[0, :N_EXPERTS]
    tiles_per = (counts + tm - 1) // tm
    tile_end = jnp.cumsum(tiles_per)
    group_start = (tile_end - tiles_per) * tm
    pos = group_start[idx[:, 0:2]] + idx[:, 2:4]
    pos_flat = pos.reshape(-1).astype(jnp.int32)
    n_tiles = (2 * t) // tm + N_EXPERTS
    n_used = tile_end[-1].astype(jnp.int32).reshape(1)
    tile_ids = jnp.minimum(jnp.arange(n_tiles, dtype=jnp.int32), n_used[0] - 1)
    tile_expert = jnp.sum(tile_ids[:, None] >= tile_end[None, :], axis=1).astype(jnp.int32)
    n_holes = n_tiles * tm - 2 * t
    hole_cnt = tiles_per * tm - counts
    hole_end = jnp.cumsum(hole_cnt)
    j = jnp.arange(n_holes, dtype=jnp.int32)
    he = jnp.sum(j[:, None] >= hole_end[None, :], axis=1)
    onehot = he[:, None] == jnp.arange(N_EXPERTS, dtype=jnp.int32)[None, :]
    in_group = jnp.sum(jnp.where(onehot, (group_start + counts - (hole_end - hole_cnt))[None, :], 0), axis=1) + j
    in_tail = n_used[0] * tm + (j - hole_end[-1])
    hole_pos = jnp.where(he < N_EXPERTS, in_group, in_tail).astype(jnp.int32)
    keys = jnp.concatenate([pos_flat, hole_pos]).astype(jnp.uint32)
    vals = jnp.concatenate([jnp.arange(2 * t, dtype=jnp.uint32) // 2, jnp.zeros((n_holes,), jnp.uint32)])
    tok_bits = max(1, (t - 1).bit_length())
    assert n_tiles * tm <= 1 << (32 - tok_bits), "row index and token index must pack into 32 bits"
    packed = lax.sort(keys * jnp.uint32(1 << tok_bits) + vals)
    tok_sorted = (packed & jnp.uint32((1 << tok_bits) - 1)).astype(jnp.int32)
    xs = _dispatch(xn, tok_sorted, tm)
    ys = _moe_experts(xs, tile_expert, n_used, wg, wu, wd, tm, fc)
    return _combine(h1, gates, ys, pos_flat, 256)


def _trunk(x, norm1_g, w_in, q_norm_g, k_norm_g, ret_gn_g, ssm_a_re, ssm_a_im, ssm_b_re, ssm_b_im,
           ssm_c_re, ssm_c_im, ssm_d, ssm_log_dt, ssm_glu_w, ssm_glu_b, w_out, norm2_g,
           ffn_w_gate, ffn_w_up, ffn_w_down, moe_router, moe_w_gate, moe_w_up, moe_w_down,
           *, tm=512, tr=1024, r_chunk=256, ffn_fc=1408, moe_tm=512, moe_fc=1792):
    batch, seq, d = x.shape
    depth = norm1_g.shape[0]
    t = batch * seq
    tabs = _rope_tables(seq, ROPE_DIM, ROPE_THETA) + _rope_tables(seq, R_QK_DIM, R_ROPE_THETA)
    h = x.reshape(t, d).astype(F32)
    for layer in range(depth):
        qa, ka, va, qr, kr, vr, gr, us = _in_proj(
            h, seq, norm1_g[layer], w_in[layer].astype(BF16), q_norm_g[layer], k_norm_g[layer], tabs, tm)
        ya = _attention(qa, ka, va, batch, seq)
        yr = _retention(qr, kr, vr, gr, ret_gn_g[layer], batch, seq, tr, r_chunk)
        mats = _ssm_matrices(ssm_a_re[layer], ssm_a_im[layer], ssm_b_re[layer], ssm_b_im[layer],
                             ssm_c_re[layer], ssm_c_im[layer], ssm_log_dt[layer])
        ys = _ssm_conv(us, mats, batch, seq)
        i = layer // 2
        router = moe_router[i] if layer % 2 == 1 else None
        outs = _out_proj(h, ya, yr, ys, us, ssm_d[layer], ssm_glu_w[layer].astype(BF16), ssm_glu_b[layer],
                         w_out[layer].astype(BF16), norm2_g[layer], router, tm)
        if layer % 2 == 0:
            h1, xn = outs
            h = _ffn(xn, h1, ffn_w_gate[i].astype(BF16), ffn_w_up[i].astype(BF16), ffn_w_down[i].astype(BF16),
                     2 * tm, ffn_fc)
        else:
            h1, xn, logits = outs
            h = _moe(xn, h1, logits, moe_w_gate[i].astype(BF16), moe_w_up[i].astype(BF16),
                     moe_w_down[i].astype(BF16), moe_tm, moe_fc)
    return h.reshape(batch, seq, d).astype(x.dtype)


def kernel(x, norm1_g, w_in, q_norm_g, k_norm_g, ret_gn_g, ssm_a_re, ssm_a_im, ssm_b_re, ssm_b_im, ssm_c_re,
           ssm_c_im, ssm_d, ssm_log_dt, ssm_glu_w, ssm_glu_b, w_out, norm2_g, ffn_w_gate, ffn_w_up, ffn_w_down,
           moe_router, moe_w_gate, moe_w_up, moe_w_down):
    return _trunk(x, norm1_g, w_in, q_norm_g, k_norm_g, ret_gn_g, ssm_a_re, ssm_a_im, ssm_b_re, ssm_b_im,
                  ssm_c_re, ssm_c_im, ssm_d, ssm_log_dt, ssm_glu_w, ssm_glu_b, w_out, norm2_g,
                  ffn_w_gate, ffn_w_up, ffn_w_down, moe_router, moe_w_gate, moe_w_up, moe_w_down)
```

```python
import functools
import math

import jax
import jax.numpy as jnp
import numpy as np
from jax import lax
from jax.experimental import pallas as pl
from jax.experimental.pallas import tpu as pltpu

F32 = jnp.float32
BF16 = jnp.bfloat16

LANES = 128
EPS = 1e-6
HEAD_DIM = 64
A_HEADS = 4
A_BLOCK = 128
A_DILATIONS = (1, 4, 16)
A_SPAN = A_BLOCK * max(A_DILATIONS)
ROPE_THETA = 500000.0
ROPE_DIM = HEAD_DIM // 4
R_HEADS = 4
R_QK_DIM = 64
R_V_DIM = 128
R_ROPE_THETA = 10000.0
S_CHANNELS = 256
S_GROUP = 16
S_GROUPS = S_CHANNELS // S_GROUP
S_STATE = 64
S_CHUNK = 32
S_PAIRS = S_GROUPS // 2
A_WIDTH = A_HEADS * HEAD_DIM
R_WIDTH = R_HEADS * R_V_DIM
N_EXPERTS = 8
NEG_BIG = -1e30

VMEM_LIMIT = 56 * 1024 * 1024


def _cparams(*sem):
    return pltpu.CompilerParams(dimension_semantics=sem, vmem_limit_bytes=VMEM_LIMIT)


def _sigmoid(x):
    return 1.0 / (1.0 + jnp.exp(-x))


def _rope_tables(seq, rot_dim, theta):
    half = rot_dim // 2
    inv = jnp.power(theta, -jnp.arange(half, dtype=F32) * 2.0 / rot_dim)
    ang = jnp.arange(seq, dtype=jnp.int32).astype(F32)[:, None] * inv[None, :]
    d = np.arange(LANES) % HEAD_DIM
    idx = jnp.asarray(d % half)
    cos = jnp.cos(ang)[:, idx]
    sin = jnp.sin(ang)[:, idx]
    in_rot = jnp.asarray(d < rot_dim)[None, :]
    first = jnp.asarray(d < half)[None, :]
    cos_t = jnp.where(in_rot, cos, 1.0)
    sin_t = jnp.where(in_rot, jnp.where(first, -sin, sin), 0.0)
    return cos_t.astype(F32), sin_t.astype(F32)


def _rope_slab(x, cos_t, sin_t, half):
    lane = lax.broadcasted_iota(jnp.int32, x.shape, 1)
    fwd = pltpu.roll(x, LANES - half, 1)
    bwd = pltpu.roll(x, half, 1)
    partner = jnp.where((lane % (2 * half)) < half, fwd, bwd)
    return x * cos_t + partner * sin_t


def _head_rms_slab(x, g):
    lane = lax.broadcasted_iota(jnp.int32, x.shape, 1)
    lo = lane < HEAD_DIM
    x2 = x * x
    s0 = jnp.sum(jnp.where(lo, x2, 0.0), axis=-1, keepdims=True)
    s1 = jnp.sum(jnp.where(lo, 0.0, x2), axis=-1, keepdims=True)
    ms = jnp.where(lo, s0, s1) * (1.0 / HEAD_DIM)
    return x * lax.rsqrt(ms + EPS) * g


def _in_proj_kernel(x_ref, g_ref, w_ref, qg_ref, kg_ref, ca_ref, sa_ref, cr_ref, sr_ref,
                    qa_ref, ka_ref, va_ref, qr_ref, kr_ref, vr_ref, gr_ref, us_ref):
    x = x_ref[...]
    ms = jnp.mean(x * x, axis=-1, keepdims=True)
    xn = (x * lax.rsqrt(ms + EPS) * g_ref[...]).astype(BF16)

    def proj(c0, n):
        return jnp.dot(xn, w_ref[:, c0:c0 + n], preferred_element_type=F32)

    ca, sa, cr, sr = ca_ref[...], sa_ref[...], cr_ref[...], sr_ref[...]
    qa = proj(0, A_WIDTH)
    ka = proj(A_WIDTH, A_WIDTH)
    for s in range(A_WIDTH // LANES):
        sl = slice(s * LANES, (s + 1) * LANES)
        qn = _rope_slab(_head_rms_slab(qa[:, sl], qg_ref[...]), ca, sa, ROPE_DIM // 2)
        qa_ref[:, sl] = qn * (math.log2(math.e) * HEAD_DIM ** -0.5)
        ka_ref[:, sl] = _rope_slab(_head_rms_slab(ka[:, sl], kg_ref[...]), ca, sa, ROPE_DIM // 2)
    va_ref[...] = proj(2 * A_WIDTH, A_WIDTH)
    c0 = 3 * A_WIDTH
    rqk = R_HEADS * R_QK_DIM
    qr = proj(c0, rqk)
    kr = proj(c0 + rqk, rqk)
    for s in range(rqk // LANES):
        sl = slice(s * LANES, (s + 1) * LANES)
        qr_ref[:, sl] = _rope_slab(qr[:, sl], cr, sr, R_QK_DIM // 2).astype(BF16)
        kr_ref[:, sl] = (_rope_slab(kr[:, sl], cr, sr, R_QK_DIM // 2) * (R_QK_DIM ** -0.5)).astype(BF16)
    c0 += 2 * rqk
    vr_ref[...] = proj(c0, R_WIDTH).astype(BF16)
    gr_ref[...] = proj(c0 + R_WIDTH, R_WIDTH).astype(BF16)
    us_ref[...] = proj(c0 + 2 * R_WIDTH, S_CHANNELS)


def _in_proj(h2d, seq, norm_g, w_in, q_g, k_g, tabs, tm):
    t, d = h2d.shape
    n_cols = w_in.shape[1]
    nt_seq = seq // tm
    row = lambda i: (i, 0)
    fixed = lambda i: (0, 0)
    tab = lambda i: (i % nt_seq, 0)
    out_shapes = (
        jax.ShapeDtypeStruct((t, A_WIDTH), F32), jax.ShapeDtypeStruct((t, A_WIDTH), F32),
        jax.ShapeDtypeStruct((t, A_WIDTH), F32),
        jax.ShapeDtypeStruct((t, R_HEADS * R_QK_DIM), BF16), jax.ShapeDtypeStruct((t, R_HEADS * R_QK_DIM), BF16),
        jax.ShapeDtypeStruct((t, R_WIDTH), BF16), jax.ShapeDtypeStruct((t, R_WIDTH), BF16),
        jax.ShapeDtypeStruct((t, S_CHANNELS), F32))
    return pl.pallas_call(
        _in_proj_kernel,
        grid=(t // tm,),
        in_specs=[pl.BlockSpec((tm, d), row), pl.BlockSpec((1, d), fixed), pl.BlockSpec((d, n_cols), fixed),
                  pl.BlockSpec((1, LANES), fixed), pl.BlockSpec((1, LANES), fixed)]
                 + [pl.BlockSpec((tm, LANES), tab)] * 4,
        out_specs=[pl.BlockSpec((tm, s.shape[1]), row) for s in out_shapes],
        out_shape=out_shapes,
        compiler_params=_cparams("arbitrary"),
        name="in_proj",
    )(h2d, norm_g.reshape(1, d), w_in, jnp.tile(q_g, 2).reshape(1, LANES), jnp.tile(k_g, 2).reshape(1, LANES), *tabs)


def _attn_kernel(q_ref, k_ref, v_ref, o_ref, kbuf, vbuf, acc, mst, lst, bias, *, unroll):
    i = pl.program_id(2)
    span = A_SPAN

    @pl.when(i == 0)
    def _():
        kbuf[0:span, :] = jnp.zeros((span, LANES), F32)
        vbuf[0:span, :] = jnp.zeros((span, LANES), F32)

    @pl.when(i > 0)
    def _():
        kbuf[0:span, :] = kbuf[span:2 * span, :]
        vbuf[0:span, :] = vbuf[span:2 * span, :]

    kbuf[span:2 * span, :] = k_ref[...]
    vbuf[span:2 * span, :] = v_ref[...]

    qi = lax.broadcasted_iota(jnp.int32, (A_BLOCK, 2 * A_BLOCK), 0)
    kj = lax.broadcasted_iota(jnp.int32, (A_BLOCK, 2 * A_BLOCK), 1)
    bias[0] = jnp.where(kj >= qi, jnp.where(kj <= qi + A_BLOCK, 0.0, NEG_BIG), NEG_BIG)
    bias[1] = jnp.where(kj >= jnp.maximum(qi, A_BLOCK), jnp.where(kj <= qi + A_BLOCK, 0.0, NEG_BIG), NEG_BIG)
    lane = lax.broadcasted_iota(jnp.int32, (A_BLOCK, LANES), 1)
    lo = lane < HEAD_DIM

    def rows(start, n, d):
        if d == 1:
            return pl.ds(pl.multiple_of(start, A_BLOCK), n)
        return pl.ds(start, n, stride=d)

    for pi, d in enumerate(A_DILATIONS):
        n_blk = span // A_BLOCK

        def body(blk, carry, d=d, pi=pi):
            if d == 1:
                sp, r = blk, 0
            elif d * A_BLOCK == span:
                sp, r = 0, blk
            else:
                sp, r = blk // d, blk % d
            q0 = sp * (A_BLOCK * d) + r
            qb = q_ref[rows(q0, A_BLOCK, d), :]
            k0 = span + q0 - A_BLOCK * d
            kb = kbuf[rows(k0, 2 * A_BLOCK, d), :].astype(BF16)
            vb = vbuf[rows(k0, 2 * A_BLOCK, d), :].astype(BF16)
            mask = bias[jnp.where(jnp.logical_or(i > 0, sp > 0), 0, 1)]
            q2 = jnp.concatenate([jnp.where(lo, qb, 0.0), jnp.where(lo, 0.0, qb)], axis=0).astype(BF16)
            s = lax.dot_general(q2, kb, (((1,), (1,)), ((), ())), preferred_element_type=F32)
            s = s + jnp.concatenate([mask, mask], axis=0)
            m_h = jnp.max(s, axis=-1, keepdims=True)
            p = jnp.exp2(s - m_h)
            l_h = jnp.sum(p, axis=-1, keepdims=True)
            o_h = jnp.dot(p.astype(BF16), vb, preferred_element_type=F32)
            qrows = rows(q0, A_BLOCK, d)
            acc[pi, qrows, :] = jnp.where(lo, o_h[:A_BLOCK], o_h[A_BLOCK:])
            mst[pi, qrows, :] = jnp.where(lo, m_h[:A_BLOCK], m_h[A_BLOCK:])
            lst[pi, qrows, :] = jnp.where(lo, l_h[:A_BLOCK], l_h[A_BLOCK:])
            return carry

        lax.fori_loop(0, n_blk, body, 0, unroll=unroll)

    n_pat = len(A_DILATIONS)
    cr = 256
    for c in range(span // cr):
        rs = slice(c * cr, (c + 1) * cr)
        m_p = [mst[pi, rs, :] for pi in range(n_pat)]
        m = functools.reduce(jnp.maximum, m_p)
        w = [jnp.exp2(mp - m) for mp in m_p]
        num = sum(w[pi] * acc[pi, rs, :] for pi in range(n_pat))
        den = sum(w[pi] * lst[pi, rs, :] for pi in range(n_pat))
        o_ref[rs, :] = (num / den).astype(o_ref.dtype)


def _attention(qa, ka, va, batch, seq, unroll=8):
    t = qa.shape[0]
    nt = seq // A_SPAN
    n_slab = A_WIDTH // LANES
    n_pat = len(A_DILATIONS)
    blk = pl.BlockSpec((A_SPAN, LANES), lambda b, s, i: (b * nt + i, s))
    return pl.pallas_call(
        functools.partial(_attn_kernel, unroll=unroll),
        grid=(batch, n_slab, nt),
        in_specs=[blk, blk, blk],
        out_specs=blk,
        out_shape=jax.ShapeDtypeStruct((t, A_WIDTH), BF16),
        scratch_shapes=[pltpu.VMEM((2 * A_SPAN, LANES), F32), pltpu.VMEM((2 * A_SPAN, LANES), F32),
                        pltpu.VMEM((n_pat, A_SPAN, LANES), F32), pltpu.VMEM((n_pat, A_SPAN, LANES), F32),
                        pltpu.VMEM((n_pat, A_SPAN, LANES), F32), pltpu.VMEM((2, A_BLOCK, 2 * A_BLOCK), F32)],
        compiler_params=_cparams("arbitrary", "arbitrary", "arbitrary"),
        name="dilated_attention",
    )(qa, ka, va)


def _retention_tables(chunk):
    log_gamma = jnp.log1p(-jnp.exp2(-5.0 - jnp.arange(R_HEADS, dtype=F32)))
    idx = jnp.arange(chunk, dtype=F32)
    diff = idx[:, None] - idx[None, :]
    decay = jnp.where(diff >= 0, jnp.exp(log_gamma[:, None, None] * jnp.maximum(diff, 0.0)), 0.0)
    zeta = jnp.exp(log_gamma[:, None] * (chunk - 1.0 - idx))
    xi = jnp.exp(log_gamma[:, None] * (idx + 1.0))
    cdec = jnp.exp(log_gamma * chunk)

    def slab(tab):
        tab = tab.reshape(R_HEADS // 2, 2, chunk)
        return jnp.repeat(tab.transpose(0, 2, 1), R_QK_DIM, axis=2)

    cdec_t = jnp.broadcast_to(cdec[:, None, None], (R_HEADS, 1, LANES))
    return decay.astype(F32), slab(zeta).astype(F32), slab(xi).astype(F32), cdec_t.astype(F32)


def _retention_kernel(q_ref, k_ref, v_ref, g_ref, gn_ref, dec_ref, zeta_ref, xi_ref, cdec_ref, o_ref, state,
                      *, chunk):
    @pl.when(pl.program_id(1) == 0)
    def _():
        state[...] = jnp.zeros(state.shape, F32)

    rows_total = q_ref.shape[0]
    lane = lax.broadcasted_iota(jnp.int32, (chunk, LANES), 1)
    lo = lane < R_QK_DIM
    for c in range(rows_total // chunk):
        rs = slice(c * chunk, (c + 1) * chunk)
        for s in range(R_HEADS // 2):
            qs = q_ref[rs, s * LANES:(s + 1) * LANES]
            ks = k_ref[rs, s * LANES:(s + 1) * LANES]
            kz = (ks.astype(F32) * zeta_ref[s]).astype(BF16)
            for hh in range(2):
                h = 2 * s + hh
                mask = lo if hh == 0 else jnp.logical_not(lo)
                qm = jnp.where(mask, qs, jnp.zeros_like(qs))
                vh = v_ref[rs, h * R_V_DIM:(h + 1) * R_V_DIM]
                sc = lax.dot_general(qm, ks, (((1,), (1,)), ((), ())), preferred_element_type=F32)
                sc = (sc * dec_ref[h]).astype(BF16)
                y = jnp.dot(sc, vh, preferred_element_type=F32)
                qx = (qm.astype(F32) * xi_ref[s]).astype(BF16)
                st = state[h]
                y = y + jnp.dot(qx, st.astype(BF16), preferred_element_type=F32)
                kv = lax.dot_general(kz, vh, (((0,), (0,)), ((), ())), preferred_element_type=F32)
                state[h] = cdec_ref[h] * st + kv
                mu = jnp.mean(y, axis=-1, keepdims=True)
                yc = y - mu
                var = jnp.mean(yc * yc, axis=-1, keepdims=True)
                yn = yc * lax.rsqrt(var + 1e-5) * gn_ref[:, h * R_V_DIM:(h + 1) * R_V_DIM]
                g = g_ref[rs, h * R_V_DIM:(h + 1) * R_V_DIM].astype(F32)
                o_ref[rs, h * R_V_DIM:(h + 1) * R_V_DIM] = (g * _sigmoid(g) * yn).astype(o_ref.dtype)


def _retention(qr, kr, vr, gr, gn_g, batch, seq, tr, chunk):
    t = qr.shape[0]
    nt = seq // tr
    decay, zeta, xi, cdec = _retention_tables(chunk)
    row = lambda b, i: (b * nt + i, 0)
    fix2 = lambda b, i: (0, 0)
    fix3 = lambda b, i: (0, 0, 0)
    rqk = R_HEADS * R_QK_DIM
    return pl.pallas_call(
        functools.partial(_retention_kernel, chunk=chunk),
        grid=(batch, nt),
        in_specs=[pl.BlockSpec((tr, rqk), row), pl.BlockSpec((tr, rqk), row),
                  pl.BlockSpec((tr, R_WIDTH), row), pl.BlockSpec((tr, R_WIDTH), row),
                  pl.BlockSpec((1, R_WIDTH), fix2),
                  pl.BlockSpec((R_HEADS, chunk, chunk), fix3),
                  pl.BlockSpec((R_HEADS // 2, chunk, LANES), fix3),
                  pl.BlockSpec((R_HEADS // 2, chunk, LANES), fix3),
                  pl.BlockSpec((R_HEADS, 1, LANES), fix3)],
        out_specs=pl.BlockSpec((tr, R_WIDTH), row),
        out_shape=jax.ShapeDtypeStruct((t, R_WIDTH), BF16),
        scratch_shapes=[pltpu.VMEM((R_HEADS, LANES, R_V_DIM), F32)],
        compiler_params=_cparams("arbitrary", "arbitrary"),
        name="retention",
    )(qr, kr, vr, gr, gn_g.reshape(1, R_WIDTH), decay, zeta, xi, cdec)


def _ssm_matrices(a_re, a_im, b_re, b_im, c_re, c_im, log_dt):
    ell = S_CHUNK
    g_n, p_n, c_n = S_GROUPS, S_STATE, S_GROUP
    dt = jnp.exp(log_dt.astype(F32))[:, None]
    lam_re, lam_im = a_re.astype(F32), a_im.astype(F32)
    mag = jnp.exp(lam_re * dt)
    abar_re = mag * jnp.cos(lam_im * dt)
    abar_im = mag * jnp.sin(lam_im * dt)
    den = lam_re * lam_re + lam_im * lam_im
    nr, ni = abar_re - 1.0, abar_im
    f_re = ((nr * lam_re + ni * lam_im) / den)[..., None]
    f_im = ((ni * lam_re - nr * lam_im) / den)[..., None]
    br, bi = b_re.astype(F32), b_im.astype(F32)
    bb_re = f_re * br - f_im * bi
    bb_im = f_re * bi + f_im * br
    j = jnp.arange(ell + 1, dtype=F32)[:, None, None]
    pw_mag = jnp.exp(j * (lam_re * dt)[None])
    pw_ang = j * (lam_im * dt)[None]
    pw_re = pw_mag * jnp.cos(pw_ang)
    pw_im = pw_mag * jnp.sin(pw_ang)
    cr, ci = c_re.astype(F32), c_im.astype(F32)
    hi = lax.Precision.HIGHEST
    w_re = cr[None] * pw_re[:, :, None, :] - ci[None] * pw_im[:, :, None, :]
    w_im = cr[None] * pw_im[:, :, None, :] + ci[None] * pw_re[:, :, None, :]
    kern = (jnp.einsum('jgcp,gpd->jgcd', w_re[:ell], bb_re, precision=hi)
            - jnp.einsum('jgcp,gpd->jgcd', w_im[:ell], bb_im, precision=hi))
    lag_rows = kern.transpose(1, 3, 2, 0).reshape(g_n, c_n, c_n * ell)
    toep = pl.pallas_call(
        _toeplitz_kernel,
        grid=(g_n,),
        in_specs=[pl.BlockSpec((1, c_n, c_n * ell), lambda g: (g, 0, 0))],
        out_specs=pl.BlockSpec((1, c_n * ell, c_n * ell), lambda g: (g, 0, 0)),
        out_shape=jax.ShapeDtypeStruct((g_n, c_n * ell, c_n * ell), BF16),
        compiler_params=_cparams("arbitrary"),
        name="ssm_toeplitz",
    )(lag_rows)
    e_re = pw_re[ell - 1 - np.arange(ell)].transpose(1, 0, 2)[:, None]
    e_im = pw_im[ell - 1 - np.arange(ell)].transpose(1, 0, 2)[:, None]
    bbt_re = bb_re.transpose(0, 2, 1)[:, :, None, :]
    bbt_im = bb_im.transpose(0, 2, 1)[:, :, None, :]
    bz_re = (e_re * bbt_re - e_im * bbt_im).reshape(g_n, c_n * ell, p_n)
    bz_im = (e_re * bbt_im + e_im * bbt_re).reshape(g_n, c_n * ell, p_n)
    zeros = jnp.zeros_like(bz_re[0::2])
    top = jnp.concatenate([bz_re[0::2], zeros, bz_im[0::2], zeros], axis=-1)
    bot = jnp.concatenate([zeros, bz_re[1::2], zeros, bz_im[1::2]], axis=-1)
    bz = jnp.concatenate([top, bot], axis=1)
    cz_re = w_re[1:].transpose(1, 3, 2, 0).reshape(g_n, p_n, c_n * ell)
    cz_im = -w_im[1:].transpose(1, 3, 2, 0).reshape(g_n, p_n, c_n * ell)
    zc = jnp.zeros_like(cz_re[0::2])
    cz = jnp.concatenate([
        jnp.concatenate([cz_re[0::2], zc], axis=-1),
        jnp.concatenate([zc, cz_re[1::2]], axis=-1),
        jnp.concatenate([cz_im[0::2], zc], axis=-1),
        jnp.concatenate([zc, cz_im[1::2]], axis=-1)], axis=1)
    al_re = pw_re[ell].reshape(S_PAIRS, 2 * p_n)
    al_im = pw_im[ell].reshape(S_PAIRS, 2 * p_n)
    a_l = jnp.stack([al_re, al_im], axis=0)
    return toep, bz.astype(BF16), cz.astype(BF16), a_l.astype(F32)


def _toeplitz_kernel(k_ref, m_ref):
    ell = S_CHUNK
    width = k_ref.shape[2]
    s_i = lax.broadcasted_iota(jnp.int32, (ell, width), 0)
    t_i = lax.broadcasted_iota(jnp.int32, (ell, width), 1) % ell
    for c in range(k_ref.shape[1]):
        row = jnp.broadcast_to(k_ref[0, c:c + 1, :], (ell, width))
        shifted = pltpu.roll(row, 0, 1, stride=1, stride_axis=0)
        m_ref[0, c * ell:(c + 1) * ell, :] = jnp.where(t_i >= s_i, shifted, 0.0).astype(m_ref.dtype)


def _ssm_state_kernel(u_ref, bz_ref, s_ref):
    s_ref[...] = jnp.dot(u_ref[...], bz_ref[0], preferred_element_type=F32)


def _ssm_scan_kernel(s_ref, al_ref, h_ref, *, batch, n_chunks):
    n_blk = 2 * S_PAIRS
    a_re = [al_ref[0, k:k + 1, :] for k in range(S_PAIRS)]
    a_im = [al_ref[1, k:k + 1, :] for k in range(S_PAIRS)]

    def body(n, carry):
        new = []
        for b in range(batch):
            row = b * n_chunks + n
            h = carry[b * n_blk:(b + 1) * n_blk]
            s_row = s_ref[pl.ds(row, 1), :]
            h_ref[pl.ds(row, 1), :] = jnp.concatenate(h, axis=1)
            for k in range(S_PAIRS):
                hr, hi = h[2 * k], h[2 * k + 1]
                sr = s_row[:, (2 * k) * LANES:(2 * k + 1) * LANES]
                si = s_row[:, (2 * k + 1) * LANES:(2 * k + 2) * LANES]
                new.append(a_re[k] * hr - a_im[k] * hi + sr)
                new.append(a_re[k] * hi + a_im[k] * hr + si)
        return tuple(new)

    init = tuple(jnp.zeros((1, LANES), F32) for _ in range(batch * n_blk))
    lax.fori_loop(0, n_chunks, body, init)


def _ssm_out_kernel(u_ref, toep_ref, h_ref, cz_ref, y_ref):
    half = S_CHUNK * S_GROUP
    cross = jnp.dot(h_ref[...].astype(BF16), cz_ref[0], preferred_element_type=F32)
    for g in range(2):
        sl = slice(g * half, (g + 1) * half)
        y = jnp.dot(u_ref[:, sl], toep_ref[g], preferred_element_type=F32) + cross[:, sl]
        y_ref[:, sl] = y.astype(y_ref.dtype)


def _ssm_conv(us, mats, batch, seq):
    toep, bz, cz, a_l = mats
    t = us.shape[0]
    ell = S_CHUNK
    n_chunks = seq // ell
    n_all = batch * n_chunks
    pair_w = 2 * ell * S_GROUP
    u_t = us.astype(BF16).reshape(n_all, ell, S_CHANNELS).transpose(0, 2, 1).reshape(n_all, S_PAIRS * pair_w)
    st_w = 4 * S_STATE
    s_all = pl.pallas_call(
        _ssm_state_kernel,
        grid=(S_PAIRS,),
        in_specs=[pl.BlockSpec((n_all, pair_w), lambda k: (0, k)),
                  pl.BlockSpec((1, pair_w, st_w), lambda k: (k, 0, 0))],
        out_specs=pl.BlockSpec((n_all, st_w), lambda k: (0, k)),
        out_shape=jax.ShapeDtypeStruct((n_all, S_PAIRS * st_w), F32),
        compiler_params=_cparams("arbitrary"),
        name="ssm_chunk_state",
    )(u_t, bz)
    h_prev = pl.pallas_call(
        functools.partial(_ssm_scan_kernel, batch=batch, n_chunks=n_chunks),
        out_shape=jax.ShapeDtypeStruct((n_all, S_PAIRS * st_w), F32),
        compiler_params=pltpu.CompilerParams(vmem_limit_bytes=VMEM_LIMIT),
        name="ssm_chunk_scan",
    )(s_all, a_l)
    y_t = pl.pallas_call(
        _ssm_out_kernel,
        grid=(S_PAIRS,),
        in_specs=[pl.BlockSpec((n_all, pair_w), lambda k: (0, k)),
                  pl.BlockSpec((2, pair_w // 2, pair_w // 2), lambda k: (k, 0, 0)),
                  pl.BlockSpec((n_all, st_w), lambda k: (0, k)),
                  pl.BlockSpec((1, st_w, pair_w), lambda k: (k, 0, 0))],
        out_specs=pl.BlockSpec((n_all, pair_w), lambda k: (0, k)),
        out_shape=jax.ShapeDtypeStruct((n_all, S_PAIRS * pair_w), BF16),
        compiler_params=_cparams("arbitrary"),
        name="ssm_chunk_out",
    )(u_t, toep, h_prev, cz)
    return y_t.reshape(n_all, S_CHANNELS, ell).transpose(0, 2, 1).reshape(t, S_CHANNELS)


def _out_proj_kernel(*refs, with_router):
    if with_router:
        (h_ref, ya_ref, yr_ref, ys_ref, us_ref, d_ref, gw_ref, gb_ref, w_ref, n2_ref, rt_ref,
         h1_ref, xn_ref, lg_ref) = refs
    else:
        (h_ref, ya_ref, yr_ref, ys_ref, us_ref, d_ref, gw_ref, gb_ref, w_ref, n2_ref,
         h1_ref, xn_ref) = refs
    y = ys_ref[...].astype(F32) + d_ref[...] * us_ref[...]
    z = 0.5 * y * (1.0 + jnp.tanh(math.sqrt(2.0 / math.pi) * (y + 0.044715 * (y * y * y))))
    gate = jnp.dot(z.astype(BF16), gw_ref[...], preferred_element_type=F32) + gb_ref[...]
    yc = (z * _sigmoid(gate)).astype(BF16)
    c1 = A_WIDTH
    c2 = A_WIDTH + R_WIDTH
    acc = jnp.dot(ya_ref[...], w_ref[0:c1, :], preferred_element_type=F32)
    acc = acc + jnp.dot(yr_ref[...], w_ref[c1:c2, :], preferred_element_type=F32)
    acc = acc + jnp.dot(yc, w_ref[c2:, :], preferred_element_type=F32)
    h1 = h_ref[...] + acc
    h1_ref[...] = h1
    ms = jnp.mean(h1 * h1, axis=-1, keepdims=True)
    xn = h1 * lax.rsqrt(ms + EPS) * n2_ref[...]
    if not with_router:
        xn_ref[...] = xn.astype(xn_ref.dtype)
    else:
        n_slab = xn.shape[1] // LANES
        for c in range(n_slab):
            xn_ref[pl.ds(c, xn.shape[0], stride=n_slab), :] = xn[:, c * LANES:(c + 1) * LANES]
        x_hi = xn.astype(BF16)
        x_lo = (xn - x_hi.astype(F32)).astype(BF16)
        both = jnp.dot(x_hi, rt_ref[...], preferred_element_type=F32)
        lg_ref[...] = (both[:, :LANES] + both[:, LANES:]
                       + jnp.dot(x_lo, rt_ref[:, :LANES], preferred_element_type=F32))


def _out_proj(h2d, ya, yr, ys, us, d_skip, glu_w, glu_b, w_out, norm2_g, router, tm):
    t, d = h2d.shape
    with_router = router is not None
    row = lambda i: (i, 0)
    fixed = lambda i: (0, 0)
    in_specs = [pl.BlockSpec((tm, d), row), pl.BlockSpec((tm, A_WIDTH), row), pl.BlockSpec((tm, R_WIDTH), row),
                pl.BlockSpec((tm, S_CHANNELS), row), pl.BlockSpec((tm, S_CHANNELS), row),
                pl.BlockSpec((1, S_CHANNELS), fixed), pl.BlockSpec((S_CHANNELS, S_CHANNELS), fixed),
                pl.BlockSpec((1, S_CHANNELS), fixed), pl.BlockSpec(w_out.shape, fixed), pl.BlockSpec((1, d), fixed)]
    args = [h2d, ya, yr, ys, us, d_skip.reshape(1, -1), glu_w, glu_b.reshape(1, -1), w_out, norm2_g.reshape(1, d)]
    out_shapes = [jax.ShapeDtypeStruct((t, d), F32), jax.ShapeDtypeStruct((t, d), BF16)]
    out_specs = [pl.BlockSpec((tm, d), row), pl.BlockSpec((tm, d), row)]
    if with_router:
        n_slab = d // LANES
        out_shapes[1] = jax.ShapeDtypeStruct((t * n_slab, LANES), F32)
        out_specs[1] = pl.BlockSpec((tm * n_slab, LANES), row)
        rt = jnp.pad(router.astype(F32), ((0, 0), (0, LANES - router.shape[1])))
        rt_hi = rt.astype(BF16)
        rt_lo = (rt - rt_hi.astype(F32)).astype(BF16)
        in_specs.append(pl.BlockSpec((d, 2 * LANES), fixed))
        args.append(jnp.concatenate([rt_hi, rt_lo], axis=1))
        out_shapes.append(jax.ShapeDtypeStruct((t, LANES), F32))
        out_specs.append(pl.BlockSpec((tm, LANES), row))
    return pl.pallas_call(
        functools.partial(_out_proj_kernel, with_router=with_router),
        grid=(t // tm,),
        in_specs=in_specs,
        out_specs=out_specs,
        out_shape=out_shapes,
        compiler_params=_cparams("arbitrary"),
        name="out_proj",
    )(*args)


def _swiglu_chunk(x, wg, wu, wd):
    hg = jnp.dot(x, wg, preferred_element_type=F32)
    hu = jnp.dot(x, wu, preferred_element_type=F32)
    a = (hg * _sigmoid(hg) * hu).astype(BF16)
    return jnp.dot(a, wd, preferred_element_type=F32)


def _ffn_kernel(x_ref, h_ref, wg_ref, wu_ref, wd_ref, o_ref):
    @pl.when(pl.program_id(1) == 0)
    def _():
        o_ref[...] = h_ref[...]

    o_ref[...] += _swiglu_chunk(x_ref[...], wg_ref[...], wu_ref[...], wd_ref[...])


def _ffn(xn, h1, wg, wu, wd, tm, fc):
    t, d = xn.shape
    f = wg.shape[1]
    return pl.pallas_call(
        _ffn_kernel,
        grid=(t // tm, f // fc),
        in_specs=[pl.BlockSpec((tm, d), lambda i, j: (i, 0)), pl.BlockSpec((tm, d), lambda i, j: (i, 0)),
                  pl.BlockSpec((d, fc), lambda i, j: (0, j)), pl.BlockSpec((d, fc), lambda i, j: (0, j)),
                  pl.BlockSpec((fc, d), lambda i, j: (j, 0))],
        out_specs=pl.BlockSpec((tm, d), lambda i, j: (i, 0)),
        out_shape=jax.ShapeDtypeStruct((t, d), F32),
        compiler_params=_cparams("arbitrary", "arbitrary"),
        name="swiglu_ffn",
    )(xn, h1, wg, wu, wd)


def _route_kernel(lg_ref, tri_ref, gate_ref, idx_ref, cnt_ref, carry):
    @pl.when(pl.program_id(0) == 0)
    def _():
        carry[...] = jnp.zeros(carry.shape, F32)

    lg = lg_ref[...]
    lane = lax.broadcasted_iota(jnp.int32, lg.shape, 1)
    valid = lane < N_EXPERTS
    mx = jnp.max(jnp.where(valid, lg, NEG_BIG), axis=-1, keepdims=True)
    ex = jnp.where(valid, jnp.exp(lg - mx), 0.0)
    probs = ex / jnp.sum(ex, axis=-1, keepdims=True)
    p1 = jnp.max(probs, axis=-1, keepdims=True)
    e1 = jnp.min(jnp.where(valid & (probs == p1), lane, LANES), axis=-1, keepdims=True)
    rest = jnp.where(valid & (lane != e1), probs, -1.0)
    p2 = jnp.max(rest, axis=-1, keepdims=True)
    e2 = jnp.min(jnp.where(rest == p2, lane, LANES), axis=-1, keepdims=True)
    den = p1 + p2
    oh1 = lane == e1
    oh2 = lane == e2
    oh = jnp.where(oh1 | oh2, 1.0, 0.0)
    cum = jnp.dot(tri_ref[...], oh.astype(BF16), preferred_element_type=F32)
    excl = cum - oh + carry[...]
    r1 = jnp.sum(jnp.where(oh1, excl, 0.0), axis=-1, keepdims=True)
    r2 = jnp.sum(jnp.where(oh2, excl, 0.0), axis=-1, keepdims=True)
    tot = carry[...] + cum[cum.shape[0] - 1:cum.shape[0], :]
    carry[...] = tot
    cnt_ref[...] = tot.astype(jnp.int32)
    gate_ref[...] = jnp.where(lane == 0, p1 / den, jnp.where(lane == 1, p2 / den, 0.0))
    idx_ref[...] = jnp.where(lane == 0, e1, jnp.where(lane == 1, e2, jnp.where(
        lane == 2, r1.astype(jnp.int32), jnp.where(lane == 3, r2.astype(jnp.int32), 0))))


def _route(logits, tm):
    t = logits.shape[0]
    tri = jnp.asarray(np.tril(np.ones((tm, tm), np.float32)), BF16)
    row = lambda i: (i, 0)
    return pl.pallas_call(
        _route_kernel,
        grid=(t // tm,),
        in_specs=[pl.BlockSpec((tm, LANES), row), pl.BlockSpec((tm, tm), lambda i: (0, 0))],
        out_specs=[pl.BlockSpec((tm, LANES), row), pl.BlockSpec((tm, LANES), row),
                   pl.BlockSpec((1, LANES), lambda i: (0, 0))],
        out_shape=[jax.ShapeDtypeStruct((t, LANES), F32), jax.ShapeDtypeStruct((t, LANES), jnp.int32),
                   jax.ShapeDtypeStruct((1, LANES), jnp.int32)],
        scratch_shapes=[pltpu.VMEM((1, LANES), F32)],
        compiler_params=_cparams("arbitrary"),
        name="moe_route",
    )(logits, tri)


DMA_UNROLL = 8
SLABS = 8


def _slab_rows(row):
    return pl.ds(pl.multiple_of(row * SLABS, SLABS), SLABS)


def _moe_kernel(te_ref, nu_ref, tok_ref, x_hbm, wg_ref, wu_ref, wd_ref, o_ref, xbuf, xb, acc, sem):
    i = pl.program_id(0)
    f = pl.program_id(1)
    tm = xb.shape[0]
    n_used = nu_ref[0]

    def issue(tile, slot):
        def body(jb, c):
            for u in range(DMA_UNROLL):
                j = jb * DMA_UNROLL + u
                tok = tok_ref[tile * tm + j]
                pltpu.make_async_copy(x_hbm.at[_slab_rows(tok)], xbuf.at[slot, _slab_rows(j)],
                                      sem.at[slot]).start(priority=u % 2)
            return c
        lax.fori_loop(0, tm // DMA_UNROLL, body, 0)

    @pl.when(i < n_used)
    def _():
        @pl.when(f == 0)
        def _():
            @pl.when(i == 0)
            def _():
                issue(0, 0)

            @pl.when(i + 1 < n_used)
            def _():
                issue(i + 1, (i + 1) % 2)

            cur = i % 2
            pltpu.make_async_copy(x_hbm.at[pl.ds(0, tm * SLABS)], xbuf.at[cur], sem.at[cur]).wait()
            for c in range(SLABS):
                xb[:, c * LANES:(c + 1) * LANES] = xbuf[cur, pl.ds(c, tm, stride=SLABS), :].astype(BF16)
            acc[...] = jnp.zeros(acc.shape, F32)

        acc[...] += _swiglu_chunk(xb[...], wg_ref[0], wu_ref[0], wd_ref[0])

        @pl.when(f == pl.num_programs(1) - 1)
        def _():
            for c in range(SLABS):
                o_ref[pl.ds(c, tm, stride=SLABS), :] = acc[:, c * LANES:(c + 1) * LANES]

    @pl.when(jnp.logical_and(i >= nu_ref[0], f == 0))
    def _():
        o_ref[...] = jnp.zeros(o_ref.shape, F32)


def _moe_experts(xn, tok_sorted, tile_expert, n_used, wg, wu, wd, tm, fc):
    d = wg.shape[1]
    f = wg.shape[2]
    nf = f // fc
    n_rows = tok_sorted.shape[0]

    def f_eff(i, j, nu):
        return jnp.where(i < nu[0], j, nf - 1)

    return pl.pallas_call(
        _moe_kernel,
        grid_spec=pltpu.PrefetchScalarGridSpec(
            num_scalar_prefetch=3,
            grid=(n_rows // tm, nf),
            in_specs=[pl.BlockSpec(memory_space=pl.ANY),
                      pl.BlockSpec((1, d, fc), lambda i, j, te, nu, tok: (te[i], 0, f_eff(i, j, nu))),
                      pl.BlockSpec((1, d, fc), lambda i, j, te, nu, tok: (te[i], 0, f_eff(i, j, nu))),
                      pl.BlockSpec((1, fc, d), lambda i, j, te, nu, tok: (te[i], f_eff(i, j, nu), 0))],
            out_specs=pl.BlockSpec((tm * SLABS, LANES), lambda i, j, te, nu, tok: (i, 0)),
            scratch_shapes=[pltpu.VMEM((2, tm * SLABS, LANES), F32), pltpu.VMEM((tm, d), BF16),
                            pltpu.VMEM((tm, d), F32), pltpu.SemaphoreType.DMA((2,))]),
        out_shape=jax.ShapeDtypeStruct((n_rows * SLABS, LANES), F32),
        compiler_params=_cparams("arbitrary", "arbitrary"),
        name="moe_experts",
    )(tile_expert, n_used, tok_sorted, xn, wg, wu, wd)


def _combine_kernel(pos_ref, h_ref, gate_ref, ys_hbm, o_ref, buf, sem, *, tc):
    i = pl.program_id(0)
    n = pl.num_programs(0)

    def issue(step, slot_buf):
        def body(jb, c):
            for u in range(DMA_UNROLL):
                j = jb * DMA_UNROLL + u
                for k in range(2):
                    src = pos_ref[2 * (step * tc + j) + k]
                    pltpu.make_async_copy(ys_hbm.at[_slab_rows(src)], buf.at[slot_buf, k, _slab_rows(j)],
                                          sem.at[slot_buf]).start(priority=k)
            return c
        lax.fori_loop(0, tc // DMA_UNROLL, body, 0)

    @pl.when(i == 0)
    def _():
        issue(0, 0)

    @pl.when(i + 1 < n)
    def _():
        issue(i + 1, (i + 1) % 2)

    cur = i % 2
    for k in range(2):
        pltpu.make_async_copy(ys_hbm.at[pl.ds(0, tc * SLABS)], buf.at[cur, k], sem.at[cur]).wait()
    g = gate_ref[...]
    for c in range(SLABS):
        cols = slice(c * LANES, (c + 1) * LANES)
        o_ref[:, cols] = (h_ref[:, cols] + g[:, 0:1] * buf[cur, 0, pl.ds(c, tc, stride=SLABS), :]
                          + g[:, 1:2] * buf[cur, 1, pl.ds(c, tc, stride=SLABS), :])


def _combine(h1, gates, ys, pos_flat, tc):
    t, d = h1.shape
    row = lambda i, p: (i, 0)
    return pl.pallas_call(
        functools.partial(_combine_kernel, tc=tc),
        grid_spec=pltpu.PrefetchScalarGridSpec(
            num_scalar_prefetch=1,
            grid=(t // tc,),
            in_specs=[pl.BlockSpec((tc, d), row), pl.BlockSpec((tc, LANES), row),
                      pl.BlockSpec(memory_space=pl.ANY)],
            out_specs=pl.BlockSpec((tc, d), row),
            scratch_shapes=[pltpu.VMEM((2, 2, tc * SLABS, LANES), F32), pltpu.SemaphoreType.DMA((2,))]),
        out_shape=jax.ShapeDtypeStruct((t, d), F32),
        compiler_params=_cparams("arbitrary"),
        name="moe_combine",
    )(pos_flat, h1, gates, ys)


def _moe(xn, h1, logits, wg, wu, wd, tm, fc):
    t = h1.shape[0]
    gates, idx, counts = _route(logits, 512)
    counts = counts[0, :N_EXPERTS]
    tiles_per = (counts + tm - 1) // tm
    tile_end = jnp.cumsum(tiles_per)
    group_start = (tile_end - tiles_per) * tm
    pos = group_start[idx[:, 0:2]] + idx[:, 2:4]
    pos_flat = pos.reshape(-1).astype(jnp.int32)
    n_tiles = (2 * t) // tm + N_EXPERTS
    n_used = tile_end[-1].astype(jnp.int32).reshape(1)
    tile_ids = jnp.minimum(jnp.arange(n_tiles, dtype=jnp.int32), n_used[0] - 1)
    tile_expert = jnp.sum(tile_ids[:, None] >= tile_end[None, :], axis=1).astype(jnp.int32)
    n_holes = n_tiles * tm - 2 * t
    hole_cnt = tiles_per * tm - counts
    hole_end = jnp.cumsum(hole_cnt)
    j = jnp.arange(n_holes, dtype=jnp.int32)
    he = jnp.sum(j[:, None] >= hole_end[None, :], axis=1)
    onehot = he[:, None] == jnp.arange(N_EXPERTS, dtype=jnp.int32)[None, :]
    in_group = jnp.sum(jnp.where(onehot, (group_start + counts - (hole_end - hole_cnt))[None, :], 0), axis=1) + j
    in_tail = n_used[0] * tm + (j - hole_end[-1])
    hole_pos = jnp.where(he < N_EXPERTS, in_group, in_tail).astype(jnp.int32)
    keys = jnp.concatenate([pos_flat, hole_pos]).astype(jnp.uint32)
    vals = jnp.concatenate([jnp.arange(2 * t, dtype=jnp.uint32) // 2, jnp.zeros((n_holes,), jnp.uint32)])
    tok_bits = max(1, (t - 1).bit_length())
    assert n_tiles * tm <= 1 << (32 - tok_bits), "row index and token index must pack into 32 bits"
    packed = lax.sort(keys * jnp.uint32(1 << tok_bits) + vals)
    tok_sorted = (packed & jnp.uint32((1 << tok_bits) - 1)).astype(jnp.int32)
    ys = _moe_experts(xn, tok_sorted, tile_expert, n_used, wg, wu, wd, tm, fc)
    return _combine(h1, gates, ys, pos_flat, 256)


def _trunk(x, norm1_g, w_in, q_norm_g, k_norm_g, ret_gn_g, ssm_a_re, ssm_a_im, ssm_b_re, ssm_b_im,
           ssm_c_re, ssm_c_im, ssm_d, ssm_log_dt, ssm_glu_w, ssm_glu_b, w_out, norm2_g,
           ffn_w_gate, ffn_w_up, ffn_w_down, moe_router, moe_w_gate, moe_w_up, moe_w_down,
           *, tm=512, tr=1024, r_chunk=256, ffn_fc=1408, moe_tm=512, moe_fc=1792):
    batch, seq, d = x.shape
    depth = norm1_g.shape[0]
    t = batch * seq
    tabs = _rope_tables(seq, ROPE_DIM, ROPE_THETA) + _rope_tables(seq, R_QK_DIM, R_ROPE_THETA)
    h = x.reshape(t, d).astype(F32)
    for layer in range(depth):
        qa, ka, va, qr, kr, vr, gr, us = _in_proj(
            h, seq, norm1_g[layer], w_in[layer].astype(BF16), q_norm_g[layer], k_norm_g[layer], tabs, tm)
        ya = _attention(qa, ka, va, batch, seq)
        yr = _retention(qr, kr, vr, gr, ret_gn_g[layer], batch, seq, tr, r_chunk)
        mats = _ssm_matrices(ssm_a_re[layer], ssm_a_im[layer], ssm_b_re[layer], ssm_b_im[layer],
                             ssm_c_re[layer], ssm_c_im[layer], ssm_log_dt[layer])
        ys = _ssm_conv(us, mats, batch, seq)
        i = layer // 2
        router = moe_router[i] if layer % 2 == 1 else None
        outs = _out_proj(h, ya, yr, ys, us, ssm_d[layer], ssm_glu_w[layer].astype(BF16), ssm_glu_b[layer],
                         w_out[layer].astype(BF16), norm2_g[layer], router, tm)
        if layer % 2 == 0:
            h1, xn = outs
            h = _ffn(xn, h1, ffn_w_gate[i].astype(BF16), ffn_w_up[i].astype(BF16), ffn_w_down[i].astype(BF16),
                     2 * tm, ffn_fc)
        else:
            h1, xn, logits = outs
            h = _moe(xn, h1, logits, moe_w_gate[i].astype(BF16), moe_w_up[i].astype(BF16),
                     moe_w_down[i].astype(BF16), moe_tm, moe_fc)
    return h.reshape(batch, seq, d).astype(x.dtype)


def kernel(x, norm1_g, w_in, q_norm_g, k_norm_g, ret_gn_g, ssm_a_re, ssm_a_im, ssm_b_re, ssm_b_im, ssm_c_re,
           ssm_c_im, ssm_d, ssm_log_dt, ssm_glu_w, ssm_glu_b, w_out, norm2_g, ffn_w_gate, ffn_w_up, ffn_w_down,
           moe_router, moe_w_gate, moe_w_up, moe_w_down):
    return _trunk(x, norm1_g, w_in, q_norm_g, k_norm_g, ret_gn_g, ssm_a_re, ssm_a_im, ssm_b_re, ssm_b_im,
                  ssm_c_re, ssm_c_im, ssm_d, ssm_log_dt, ssm_glu_w, ssm_glu_b, w_out, norm2_g,
                  ffn_w_gate, ffn_w_up, ffn_w_down, moe_router, moe_w_gate, moe_w_up, moe_w_down)
```
